```python
import math
import jax
import jax.numpy as jnp
from jax import lax
import numpy as np


D_MODEL = 2048
BATCH = 4
SEQ = 2048
DEPTH = 4

GRID_W = 64
CTX_LEN = 256
EPS = 1e-6
Q_BLOCK = 128

MLA_HEADS = 8
MLA_NOPE = 128
MLA_ROPE = 64
MLA_V = 128
MLA_Q_RANK = 512
MLA_KV_RANK = 256
ROPE_BASE = 10000.0

SSM_HEADS = 16
SSM_HEAD_DIM = 64
SSM_INNER = SSM_HEADS * SSM_HEAD_DIM
SSM_GROUPS = 4
SSM_STATE = 128
SSM_CONV = 5
SSM_CHUNK = 128
SSM_XBC = SSM_INNER + 2 * SSM_GROUPS * SSM_STATE

NA_HEADS = 8
NA_HEAD_DIM = 128
NA_ROWS = 8
NA_COLS = 16
NA_WIDTH = NA_HEADS * NA_HEAD_DIM

ML_HEADS = 4
ML_QK = 128
ML_V = 256
ML_CHUNK = 128

N_EXPERTS = 32
TOP_K = 4
D_EXPERT = 512
SWIGLU_LIMIT = 7.0
SWIGLU_ALPHA = 1.702

EV_SPLITS = (MLA_Q_RANK + MLA_KV_RANK + MLA_ROPE, SSM_INNER, SSM_XBC, 2 * SSM_HEADS)
EV_COLS = sum(EV_SPLITS)
EV_MIX = MLA_HEADS * MLA_V + SSM_INNER
OD_SPLITS = (NA_WIDTH, NA_WIDTH, NA_WIDTH, ML_HEADS * ML_QK, ML_HEADS * ML_QK, ML_HEADS * ML_V, ML_HEADS * ML_V, 4 * ML_HEADS)
OD_COLS = sum(OD_SPLITS)
OD_MIX = NA_WIDTH + ML_HEADS * ML_V

kernel_name = 'hybrid_dit_mla_ssd_natten_mlstm_moe'


def rms_norm(x, g):
    xf = x.astype(jnp.float32)
    y = xf * lax.rsqrt(jnp.mean(xf * xf, axis=-1, keepdims=True) + EPS)
    return (y * g.astype(jnp.float32)).astype(x.dtype)


def modulate(x, g, shift, scale):
    return rms_norm(x, g) * (1 + scale) + shift


def split_cols(u, sizes):
    return jnp.split(u, [int(s) for s in np.cumsum(sizes)[:-1]], axis=-1)


def axial_rope_angles(n_tokens):
    pos = jnp.arange(n_tokens)
    row = (pos // GRID_W).astype(jnp.float32)
    col = (pos % GRID_W).astype(jnp.float32)
    axis_dim = MLA_ROPE // 2
    inv_freq = ROPE_BASE ** (-jnp.arange(0, axis_dim, 2, dtype=jnp.float32) / axis_dim)
    return row[:, None] * inv_freq, col[:, None] * inv_freq


def rotate(x, ang):
    half = x.shape[-1] // 2
    cos = jnp.cos(ang)[:, None, :].astype(x.dtype)
    sin = jnp.sin(ang)[:, None, :].astype(x.dtype)
    x1, x2 = x[..., :half], x[..., half:]
    return jnp.concatenate([x1 * cos - x2 * sin, x2 * cos + x1 * sin], axis=-1)


def rope_tail(t, ang_r, ang_c):
    a = MLA_NOPE
    m = MLA_NOPE + MLA_ROPE // 2
    return jnp.concatenate([t[..., :a], rotate(t[..., a:m], ang_r), rotate(t[..., m:], ang_c)], axis=-1)


def dense_attention(q, k, v, scale):
    s = jnp.einsum('bqhd,bkhd->bhqk', q, k).astype(jnp.float32) * scale
    p = jax.nn.softmax(s, axis=-1).astype(v.dtype)
    return jnp.einsum('bhqk,bkhd->bqhd', p, v)


def block_attention(q, k, v, scale):
    B, S, H, D = q.shape
    qb = q.reshape(B, S // Q_BLOCK, Q_BLOCK, H, D).swapaxes(0, 1)
    o = lax.map(lambda qblk: dense_attention(qblk, k, v, scale), qb)
    return o.swapaxes(0, 1).reshape(B, S, H, v.shape[-1])


def mla_heads(u, ln_q, ln_kv, w_uq, w_ukv, g_q, g_k):
    B, N, _ = u.shape
    cq, ckv, kpe = split_cols(u, (MLA_Q_RANK, MLA_KV_RANK, MLA_ROPE))
    q = (rms_norm(cq, ln_q) @ w_uq).reshape(B, N, MLA_HEADS, MLA_NOPE + MLA_ROPE)
    kv = (rms_norm(ckv, ln_kv) @ w_ukv).reshape(B, N, MLA_HEADS, MLA_NOPE + MLA_V)
    k_pe = jnp.broadcast_to(kpe[:, :, None, :], (B, N, MLA_HEADS, MLA_ROPE))
    k = jnp.concatenate([kv[..., :MLA_NOPE], k_pe], axis=-1)
    return rms_norm(q, g_q), rms_norm(k, g_k), kv[..., MLA_NOPE:]


def depthwise_conv(x, w, b):
    K, C = w.shape
    y = lax.conv_general_dilated(x, w[:, None, :].astype(x.dtype), (1,), [(K // 2, K // 2)],
                                 dimension_numbers=('NWC', 'WIO', 'NWC'), feature_group_count=C)
    return y + b


def ssm_inputs(xbc, dt_raw, conv_w, conv_b, dt_bias):
    B, N, _ = xbc.shape
    xbc = jax.nn.silu(depthwise_conv(xbc, conv_w, conv_b)).astype(jnp.float32)
    xs, bm, cm = split_cols(xbc, (SSM_INNER, SSM_GROUPS * SSM_STATE, SSM_GROUPS * SSM_STATE))
    dt = jax.nn.softplus(dt_raw.astype(jnp.float32).reshape(B, N, 2, SSM_HEADS) + dt_bias.astype(jnp.float32))
    return (xs.reshape(B, N, SSM_HEADS, SSM_HEAD_DIM), bm.reshape(B, N, SSM_GROUPS, SSM_STATE),
            cm.reshape(B, N, SSM_GROUPS, SSM_STATE), dt)


def segsum(a):
    T = a.shape[-1]
    cs = jnp.cumsum(a, axis=-1)
    diff = cs[..., :, None] - cs[..., None, :]
    return jnp.where(jnp.tril(jnp.ones((T, T), dtype=bool)), diff, -jnp.inf)


def ssd_scan(xs, dt, A, bm, cm, h0):
    b, n, H, P = xs.shape
    G, Ns = bm.shape[2], bm.shape[3]
    R = H // G
    L = SSM_CHUNK
    nc = n // L
    xg = (xs * dt[..., None]).reshape(b, nc, L, G, R, P)
    a = (dt * A).reshape(b, nc, L, G, R).transpose(0, 3, 4, 1, 2)
    bc = bm.reshape(b, nc, L, G, Ns)
    cc = cm.reshape(b, nc, L, G, Ns)
    a_cs = jnp.cumsum(a, axis=-1)
    lmat = jnp.exp(segsum(a))
    y_diag = jnp.einsum('bclgn,bcsgn,bgrcls,bcsgrp->bclgrp', cc, bc, lmat, xg)
    decay_states = jnp.exp(a_cs[..., -1:] - a_cs)
    states = jnp.einsum('bclgn,bgrcl,bclgrp->bcgrpn', bc, decay_states, xg)
    states = jnp.concatenate([h0.reshape(b, G, R, P, Ns)[:, None], states], axis=1)
    chunk_a = jnp.pad(a_cs[..., -1], ((0, 0), (0, 0), (0, 0), (1, 0)))
    decay_chunk = jnp.exp(segsum(chunk_a))
    new_states = jnp.einsum('bgrzc,bcgrpn->bzgrpn', decay_chunk, states)
    start_states, h_final = new_states[:, :-1], new_states[:, -1]
    y_off = jnp.einsum('bclgn,bcgrpn,bgrcl->bclgrp', cc, start_states, jnp.exp(a_cs))
    return (y_diag + y_off).reshape(b, n, H, P), h_final.reshape(b, H, P, Ns)


def ssd_bidir(xs, bm, cm, dt, A, states):
    flip = lambda t: jnp.flip(t, axis=1)
    y_f, h_f = ssd_scan(xs, dt[:, :, 0], A[0], bm, cm, states[0])
    y_b, h_b = ssd_scan(flip(xs), flip(dt[:, :, 1]), A[1], flip(bm), flip(cm), states[1])
    return y_f + flip(y_b), (h_f, h_b)


def ssm_output(y, xs, z, d_skip, g):
    B, N = z.shape[:2]
    y = (y + d_skip.astype(jnp.float32)[:, None] * xs).reshape(B, N, SSM_INNER)
    return rms_norm(y * jax.nn.silu(z.astype(jnp.float32)), g).astype(z.dtype)


def even_mixer(h, hc, w_in, w_out, ln_q, ln_kv, w_uq, w_ukv, g_q, g_k,
               conv_w, conv_b, a_log, dt_bias, d_skip, ssm_g, ang_r, ang_c, with_ctx):
    B, S, _ = h.shape
    n_ctx = hc.shape[1]
    mla_u, z, xbc, dt_raw = split_cols(h @ w_in, EV_SPLITS)
    mla_uc, zc, xbcc, dt_rawc = split_cols(hc @ w_in, EV_SPLITS)
    scale = (MLA_NOPE + MLA_ROPE) ** -0.5
    q, k, v = mla_heads(mla_u, ln_q, ln_kv, w_uq, w_ukv, g_q, g_k)
    q, k = rope_tail(q, ang_r, ang_c), rope_tail(k, ang_r, ang_c)
    qc, kc, vc = mla_heads(mla_uc, ln_q, ln_kv, w_uq, w_ukv, g_q, g_k)
    o_mla = block_attention(q, jnp.concatenate([kc, k], axis=1), jnp.concatenate([vc, v], axis=1), scale)
    A = -jnp.exp(a_log.astype(jnp.float32))
    xs_c, b_c, c_c, dt_c = ssm_inputs(xbcc, dt_rawc, conv_w, conv_b, dt_bias)
    h0 = jnp.zeros((B, SSM_HEADS, SSM_HEAD_DIM, SSM_STATE), jnp.float32)
    y_c, ctx_states = ssd_bidir(xs_c, b_c, c_c, dt_c, A, (h0, h0))
    xs, b_l, c_l, dt_l = ssm_inputs(xbc, dt_raw, conv_w, conv_b, dt_bias)
    y_l, _ = ssd_bidir(xs, b_l, c_l, dt_l, A, ctx_states)
    y = jnp.concatenate([o_mla.reshape(B, S, -1), ssm_output(y_l, xs, z, d_skip, ssm_g)], axis=-1) @ w_out
    if not with_ctx:
        return y, None
    o_mla_c = dense_attention(qc, kc, vc, scale).reshape(B, n_ctx, -1)
    yc = jnp.concatenate([o_mla_c, ssm_output(y_c, xs_c, zc, d_skip, ssm_g)], axis=-1) @ w_out
    return y, yc


def neighbourhood_attention(q, k, v, k_ctx, v_ctx, rpb):
    B, S, H, Dh = q.shape
    rows = S // GRID_W
    kr, kc = min(NA_ROWS, rows), min(NA_COLS, GRID_W)
    n_loc = kr * GRID_W
    scale = Dh ** -0.5
    kg = k.reshape(B, rows, GRID_W, H, Dh)
    vg = v.reshape(B, rows, GRID_W, H, Dh)
    q_rows = q.reshape(B, rows, GRID_W, H, Dh).swapaxes(0, 1)
    row_ids = jnp.arange(rows)
    row_start = jnp.clip(row_ids - kr // 2, 0, rows - kr)
    col = jnp.arange(GRID_W)
    col_start = jnp.clip(col - kc // 2, 0, GRID_W - kc)
    col_ok = (col[None, :] >= col_start[:, None]) & (col[None, :] < col_start[:, None] + kc)
    key_ok = jnp.broadcast_to(col_ok[:, None, :], (GRID_W, kr, GRID_W)).reshape(GRID_W, n_loc)
    dc = jnp.clip(col[None, :] - col[:, None], -(NA_COLS - 1), NA_COLS - 1) + NA_COLS - 1

    def one_row(args):
        q_r, r, r0 = args
        k_b = lax.dynamic_slice_in_dim(kg, r0, kr, axis=1).reshape(B, n_loc, H, Dh)
        v_b = lax.dynamic_slice_in_dim(vg, r0, kr, axis=1).reshape(B, n_loc, H, Dh)
        dr = r0 + jnp.arange(kr) - r + NA_ROWS - 1
        bias = rpb[:, dr][:, :, dc]
        bias = bias.transpose(0, 2, 1, 3).reshape(H, GRID_W, n_loc).astype(jnp.float32)
        s_loc = jnp.einsum('bqhd,bkhd->bhqk', q_r, k_b).astype(jnp.float32) * scale + bias
        s_loc = jnp.where(key_ok, s_loc, -jnp.inf)
        s_ctx = jnp.einsum('bqhd,bkhd->bhqk', q_r, k_ctx).astype(jnp.float32) * scale
        p = jax.nn.softmax(jnp.concatenate([s_loc, s_ctx], axis=-1), axis=-1).astype(v.dtype)
        return (jnp.einsum('bhqk,bkhd->bqhd', p[..., :n_loc], v_b)
                + jnp.einsum('bhqk,bkhd->bqhd', p[..., n_loc:], v_ctx))

    o = lax.map(one_row, (q_rows, row_ids, row_start))
    return o.swapaxes(0, 1).reshape(B, S, H, Dh)


def na_heads(qa, ka, va, g_q, g_k):
    B, N, _ = qa.shape
    sh = (B, N, NA_HEADS, NA_HEAD_DIM)
    return rms_norm(qa.reshape(sh), g_q), rms_norm(ka.reshape(sh), g_k), va.reshape(sh)


def mlstm_scan(q, k, v, li, lf, state):
    B, N, H, _ = q.shape
    nc = N // ML_CHUNK
    chunks = lambda t: t.reshape(B, nc, ML_CHUNK, *t.shape[2:]).swapaxes(0, 1)
    mask = jnp.tril(jnp.ones((ML_CHUNK, ML_CHUNK), dtype=bool))[None, :, :, None]

    def step(carry, blk):
        C, n, m = carry
        qc, kc, vc, lic, lfc = blk
        b = jnp.cumsum(lfc, axis=1)
        d = jnp.where(mask, b[:, :, None, :] - b[:, None, :, :] + lic[:, None, :, :], -jnp.inf)
        inter = b + m[:, None, :]
        m_t = jnp.maximum(inter, jnp.max(d, axis=2))
        s = jnp.einsum('bthd,bshd->btsh', qc, kc) * jnp.exp(d - m_t[:, :, None, :])
        w_inter = jnp.exp(inter - m_t)
        num = jnp.einsum('btsh,bshv->bthv', s, vc) + w_inter[..., None] * jnp.einsum('bhvd,bthd->bthv', C, qc)
        den = jnp.sum(s, axis=2) + w_inter * jnp.einsum('bhd,bthd->bth', n, qc)
        h = num / jnp.maximum(jnp.abs(den), jnp.exp(-m_t))[..., None]
        b_end = b[:, -1]
        g = b_end[:, None, :] - b + lic
        m_new = jnp.maximum(b_end + m, jnp.max(g, axis=1))
        wk = jnp.exp(g - m_new[:, None, :])
        decay = jnp.exp(b_end + m - m_new)
        C = decay[:, :, None, None] * C + jnp.einsum('blh,blhv,blhd->bhvd', wk, vc, kc)
        n = decay[:, :, None] * n + jnp.einsum('blh,blhd->bhd', wk, kc)
        return (C, n, m_new), h

    state, h = lax.scan(step, state, (chunks(q), chunks(k), chunks(v), chunks(li), chunks(lf)))
    return h.swapaxes(0, 1).reshape(B, N, H, -1), state


def mlstm_bidir(q, k, v, ig, lf, states):
    flip = lambda t: jnp.flip(t, axis=1)
    h_f, s_f = mlstm_scan(q, k, v, ig[:, :, 0], lf[:, :, 0], states[0])
    h_b, s_b = mlstm_scan(flip(q), flip(k), flip(v), flip(ig[:, :, 1]), flip(lf[:, :, 1]), states[1])
    return h_f + flip(h_b), (s_f, s_b)


def mlstm_inputs(qm, km, vm, gates, gate_b):
    B, N, _ = qm.shape
    f32 = jnp.float32
    q = qm.astype(f32).reshape(B, N, ML_HEADS, ML_QK)
    k = km.astype(f32).reshape(B, N, ML_HEADS, ML_QK) * ML_QK ** -0.5
    v = vm.astype(f32).reshape(B, N, ML_HEADS, ML_V)
    g = gates.astype(f32).reshape(B, N, 2, 2, ML_HEADS) + gate_b.astype(f32)
    return q, k, v, g[:, :, 0], jax.nn.log_sigmoid(g[:, :, 1])


def mlstm_zero_state(B):
    f32 = jnp.float32
    z = (jnp.zeros((B, ML_HEADS, ML_V, ML_QK), f32), jnp.zeros((B, ML_HEADS, ML_QK), f32), jnp.zeros((B, ML_HEADS), f32))
    return (z, z)


def mlstm_output(h, o_raw, g):
    B, N = o_raw.shape[:2]
    h = rms_norm(h, g.reshape(ML_HEADS, ML_V))
    o = jax.nn.sigmoid(o_raw.astype(jnp.float32)).reshape(B, N, ML_HEADS, ML_V)
    return (h * o).reshape(B, N, -1).astype(o_raw.dtype)


def odd_mixer(h, hc, w_in, w_out, na_qn, na_kn, rpb, gate_b, ml_g, with_ctx):
    B, S, _ = h.shape
    n_ctx = hc.shape[1]
    qa, ka, va, qm, km, vm, om, gm = split_cols(h @ w_in, OD_SPLITS)
    qac, kac, vac, qmc, kmc, vmc, omc, gmc = split_cols(hc @ w_in, OD_SPLITS)
    q, k, v = na_heads(qa, ka, va, na_qn, na_kn)
    qc, kc, vc = na_heads(qac, kac, vac, na_qn, na_kn)
    o_na = neighbourhood_attention(q, k, v, kc, vc, rpb).reshape(B, S, NA_WIDTH)
    mq_c, mk_c, mv_c, ig_c, lf_c = mlstm_inputs(qmc, kmc, vmc, gmc, gate_b)
    hs_c, ctx_states = mlstm_bidir(mq_c, mk_c, mv_c, ig_c, lf_c, mlstm_zero_state(B))
    mq, mk, mv, ig, lf = mlstm_inputs(qm, km, vm, gm, gate_b)
    hs, _ = mlstm_bidir(mq, mk, mv, ig, lf, ctx_states)
    y = jnp.concatenate([o_na, mlstm_output(hs, om, ml_g)], axis=-1) @ w_out
    if not with_ctx:
        return y, None
    o_na_c = dense_attention(qc, kc, vc, NA_HEAD_DIM ** -0.5).reshape(B, n_ctx, NA_WIDTH)
    yc = jnp.concatenate([o_na_c, mlstm_output(hs_c, omc, ml_g)], axis=-1) @ w_out
    return y, yc


def clamped_swiglu(u):
    glu, lin = u[..., ::2], u[..., 1::2]
    glu = jnp.minimum(glu, SWIGLU_LIMIT)
    lin = jnp.clip(lin, -SWIGLU_LIMIT, SWIGLU_LIMIT)
    return glu * jax.nn.sigmoid(SWIGLU_ALPHA * glu) * (lin + 1)


def moe_ffn(h, router_w, router_b, w1, b1, w2, b2):
    logits = (h @ router_w + router_b).astype(jnp.float32)
    top_v, top_i = lax.top_k(logits, TOP_K)
    gates = jax.nn.softmax(top_v, axis=-1)
    combine = jnp.einsum('tk,tke->te', gates, jax.nn.one_hot(top_i, N_EXPERTS, dtype=jnp.float32)).astype(h.dtype)
    out = jnp.zeros_like(h)
    for e in range(N_EXPERTS):
        y = clamped_swiglu(h @ w1[e] + b1[e]) @ w2[e] + b2[e]
        out = out + combine[:, e:e + 1] * y
    return out


def setup_inputs(seed: int = 0) -> dict:
    key = jax.random.key(seed)
    keys = iter(jax.random.split(key, 48))
    f32 = jnp.float32

    def normal(shape, scale):
        return jax.random.normal(next(keys), shape, f32) * scale

    def gain(shape):
        return 1.0 + normal(shape, 0.1)

    def uniform(shape, lo, hi):
        return jax.random.uniform(next(keys), shape, f32, lo, hi)

    ne, no = (DEPTH + 1) // 2, DEPTH // 2
    D = D_MODEL
    x = normal((BATCH, SEQ, D), 1.0)
    c = normal((BATCH, D), 1.0)
    ctx = normal((BATCH, CTX_LEN, D), 1.0)
    c_ctx = normal((D,), 1.0)
    mod_w = normal((DEPTH, D, 6 * D), 0.5 * D ** -0.5)
    mod_b = normal((DEPTH, 6 * D), 0.02)
    norm_mix = gain((DEPTH, D))
    norm_ffn = gain((DEPTH, D))
    dt0 = jnp.exp(uniform((ne, 2, SSM_HEADS), math.log(1e-3), math.log(1e-1)))
    return {
        'x': x, 'c': c, 'ctx': ctx, 'c_ctx': c_ctx,
        'mod_w': mod_w, 'mod_b': mod_b, 'norm_mix': norm_mix, 'norm_ffn': norm_ffn,
        'ev_w_in': normal((ne, D, EV_COLS), D ** -0.5),
        'ev_w_out': normal((ne, EV_MIX, D), EV_MIX ** -0.5),
        'mla_ln_q': gain((ne, MLA_Q_RANK)),
        'mla_ln_kv': gain((ne, MLA_KV_RANK)),
        'mla_w_uq': normal((ne, MLA_Q_RANK, MLA_HEADS * (MLA_NOPE + MLA_ROPE)), MLA_Q_RANK ** -0.5),
        'mla_w_ukv': normal((ne, MLA_KV_RANK, MLA_HEADS * (MLA_NOPE + MLA_V)), MLA_KV_RANK ** -0.5),
        'mla_q_norm': gain((ne, MLA_NOPE + MLA_ROPE)),
        'mla_k_norm': gain((ne, MLA_NOPE + MLA_ROPE)),
        'ssm_conv_w': normal((ne, SSM_CONV, SSM_XBC), SSM_CONV ** -0.5),
        'ssm_conv_b': normal((ne, SSM_XBC), 0.02),
        'ssm_a_log': jnp.log(uniform((ne, 2, SSM_HEADS), 1.0, 16.0)),
        'ssm_dt_bias': dt0 + jnp.log(-jnp.expm1(-dt0)),
        'ssm_d': gain((ne, SSM_HEADS)),
        'ssm_norm': gain((ne, SSM_INNER)),
        'od_w_in': normal((no, D, OD_COLS), D ** -0.5),
        'od_w_out': normal((no, OD_MIX, D), OD_MIX ** -0.5),
        'na_q_norm': gain((no, NA_HEAD_DIM)),
        'na_k_norm': gain((no, NA_HEAD_DIM)),
        'na_rpb': normal((no, NA_HEADS, 2 * NA_ROWS - 1, 2 * NA_COLS - 1), 0.1),
        'ml_gate_b': jnp.concatenate([normal((no, 1, 2, ML_HEADS), 0.1),
                                      3.0 + normal((no, 1, 2, ML_HEADS), 0.5)], axis=1),
        'ml_norm': gain((no, ML_HEADS * ML_V)),
        'router_w': normal((DEPTH, D, N_EXPERTS), D ** -0.5),
        'router_b': normal((DEPTH, N_EXPERTS), 0.01),
        'moe_w1': normal((DEPTH, N_EXPERTS, D, 2 * D_EXPERT), D ** -0.5),
        'moe_b1': normal((DEPTH, N_EXPERTS, 2 * D_EXPERT), 0.02),
        'moe_w2': normal((DEPTH, N_EXPERTS, D_EXPERT, D), D_EXPERT ** -0.5),
        'moe_b2': normal((DEPTH, N_EXPERTS, D), 0.02),
    }


def reference(x, c, ctx, c_ctx, mod_w, mod_b, norm_mix, norm_ffn,
              ev_w_in, ev_w_out, mla_ln_q, mla_ln_kv, mla_w_uq, mla_w_ukv, mla_q_norm, mla_k_norm,
              ssm_conv_w, ssm_conv_b, ssm_a_log, ssm_dt_bias, ssm_d, ssm_norm,
              od_w_in, od_w_out, na_q_norm, na_k_norm, na_rpb, ml_gate_b, ml_norm,
              router_w, router_b, moe_w1, moe_b1, moe_w2, moe_b2):
    B, S, D = x.shape
    n_ctx = ctx.shape[1]
    ang_r, ang_c = axial_rope_angles(S)
    s_lat = jax.nn.silu(c)
    s_ctx = jax.nn.silu(c_ctx)
    for l in range(DEPTH):
        with_ctx = l < DEPTH - 1
        i = l // 2
        mod = jnp.split(s_lat @ mod_w[l] + mod_b[l], 6, axis=-1)
        sh_m, sc_m, gt_m, sh_f, sc_f, gt_f = [m[:, None, :] for m in mod]
        shc_m, scc_m, gtc_m, shc_f, scc_f, gtc_f = jnp.split(s_ctx @ mod_w[l] + mod_b[l], 6, axis=-1)
        h = modulate(x, norm_mix[l], sh_m, sc_m)
        hc = modulate(ctx, norm_mix[l], shc_m, scc_m)
        if l % 2 == 0:
            y, yc = even_mixer(h, hc, ev_w_in[i], ev_w_out[i], mla_ln_q[i], mla_ln_kv[i], mla_w_uq[i], mla_w_ukv[i],
                               mla_q_norm[i], mla_k_norm[i], ssm_conv_w[i], ssm_conv_b[i], ssm_a_log[i],
                               ssm_dt_bias[i], ssm_d[i], ssm_norm[i], ang_r, ang_c, with_ctx)
        else:
            y, yc = odd_mixer(h, hc, od_w_in[i], od_w_out[i], na_q_norm[i], na_k_norm[i], na_rpb[i],
                              ml_gate_b[i], ml_norm[i], with_ctx)
        x = x + gt_m * y
        h = modulate(x, norm_ffn[l], sh_f, sc_f)
        if with_ctx:
            ctx = ctx + gtc_m * yc
            hc = modulate(ctx, norm_ffn[l], shc_f, scc_f)
            tokens = jnp.concatenate([hc, h], axis=1).reshape(-1, D)
            out = moe_ffn(tokens, router_w[l], router_b[l], moe_w1[l], moe_b1[l], moe_w2[l], moe_b2[l])
            out = out.reshape(B, n_ctx + S, D)
            ctx = ctx + gtc_f * out[:, :n_ctx]
            y = out[:, n_ctx:]
        else:
            y = moe_ffn(h.reshape(-1, D), router_w[l], router_b[l], moe_w1[l], moe_b1[l],
                        moe_w2[l], moe_b2[l]).reshape(B, S, D)
        x = x + gt_f * y
    return x
```

```python
import functools
import math

import numpy as np
import jax
import jax.numpy as jnp
from jax import lax
from jax.experimental import pallas as pl
from jax.experimental.pallas import tpu as pltpu

F32 = jnp.float32
BF16 = jnp.bfloat16
HIGHEST = lax.Precision.HIGHEST

D_MODEL = 2048
DEPTH = 4
GRID_W = 64
CTX_LEN = 256
EPS = 1e-6
NEG = -1e30

MLA_HEADS = 8
MLA_NOPE = 128
MLA_ROPE = 64
MLA_V = 128
MLA_Q_RANK = 512
MLA_KV_RANK = 256
MLA_QK = MLA_NOPE + MLA_ROPE
ROPE_BASE = 10000.0

SSM_HEADS = 16
SSM_HEAD_DIM = 64
SSM_INNER = SSM_HEADS * SSM_HEAD_DIM
SSM_GROUPS = 4
SSM_STATE = 128
SSM_CONV = 5
SSM_XBC = SSM_INNER + 2 * SSM_GROUPS * SSM_STATE
CHUNK = 128

NA_HEADS = 8
NA_HEAD_DIM = 128
NA_ROWS = 8
NA_COLS = 16
NA_WIDTH = NA_HEADS * NA_HEAD_DIM

ML_HEADS = 4
ML_QK = 128
ML_V = 256

N_EXPERTS = 32
TOP_K = 4
D_EXPERT = 512
SWIGLU_LIMIT = 7.0
SWIGLU_ALPHA = 1.702

LANE = 128

EV_Z = 0
EV_XBC = 1024
EV_MLA = 3072
EV_DT = 3968
EV_PAD_COLS = 4096
OD_GATE = 6144
OD_PAD_COLS = 6272

VMEM_LIMIT = 56 * 1024 * 1024


def _cparams(sem):
    return pltpu.CompilerParams(dimension_semantics=sem, vmem_limit_bytes=VMEM_LIMIT)


def _rms(x, axis=-1):
    return x * lax.rsqrt(jnp.mean(x * x, axis=axis, keepdims=True) + EPS)


def _softplus(x):
    return jnp.maximum(x, 0.0) + jnp.log(1.0 + jnp.exp(-jnp.abs(x)))


def _sigmoid(x):
    return 1.0 / (1.0 + jnp.exp(-x))


def _dot(a, b):
    return jnp.dot(a, b, preferred_element_type=F32)


def _dot_nt(a, b):
    return lax.dot_general(a, b, (((1,), (1,)), ((), ())), preferred_element_type=F32)


def _mods_kernel(a_ref, w_ref, b_ref, o_ref):
    a = a_ref[...]
    a = a * _sigmoid(a)
    o_ref[0] = _dot(a.astype(BF16), w_ref[0].astype(BF16)) + b_ref[0]


def _mods(cond, mod_w, mod_b):
    depth, d, n = mod_w.shape
    tn = 1024
    return pl.pallas_call(
        _mods_kernel,
        grid=(depth, n // tn),
        in_specs=[
            pl.BlockSpec((8, d), lambda l, j: (0, 0)),
            pl.BlockSpec((1, d, tn), lambda l, j: (l, 0, j)),
            pl.BlockSpec((1, 1, tn), lambda l, j: (l, 0, j)),
        ],
        out_specs=pl.BlockSpec((1, 8, tn), lambda l, j: (l, 0, j)),
        out_shape=jax.ShapeDtypeStruct((depth, 8, n), F32),
        compiler_params=_cparams(("parallel", "parallel")),
        name="mods",
    )(cond, mod_w, mod_b.reshape(depth, 1, n))


def _pick_mod(i, tiles_per_batch, tm, mc, ml):
    row = lax.broadcasted_iota(jnp.int32, (tm, 1), 0)
    is_ctx = jnp.logical_and(i % tiles_per_batch == 0, row < CTX_LEN)
    return jnp.where(is_ctx, mc, ml)


def _modulated(x, g, i, tiles_per_batch, mc_ref, ml_ref, sh_idx, sc_idx):
    tm = x.shape[0]
    y = _rms(x) * g
    shift = _pick_mod(i, tiles_per_batch, tm, mc_ref[sh_idx:sh_idx + 1, :], ml_ref[0, sh_idx:sh_idx + 1, :])
    scale = _pick_mod(i, tiles_per_batch, tm, mc_ref[sc_idx:sc_idx + 1, :], ml_ref[0, sc_idx:sc_idx + 1, :])
    return y * (1.0 + scale) + shift


def _modmm_kernel(x_ref, g_ref, mc_ref, ml_ref, w_ref, o_ref, h_scr, *, tiles_per_batch, sh_idx, sc_idx):
    i = pl.program_id(0)

    @pl.when(pl.program_id(1) == 0)
    def _():
        h = _modulated(x_ref[...], g_ref[...], i, tiles_per_batch, mc_ref, ml_ref, sh_idx, sc_idx)
        h_scr[...] = h.astype(BF16)

    o_ref[...] = _dot(h_scr[...], w_ref[...])


def _modmm(xs, g, mod_c, mod_l, w, *, nt, sh_idx, sc_idx, tn):
    t, d = xs.shape
    n = w.shape[1]
    tm = 768
    tpb = nt // tm
    return pl.pallas_call(
        functools.partial(_modmm_kernel, tiles_per_batch=tpb, sh_idx=sh_idx, sc_idx=sc_idx),
        grid=(t // tm, n // tn),
        in_specs=[
            pl.BlockSpec((tm, d), lambda i, j: (i, 0)),
            pl.BlockSpec((1, d), lambda i, j: (0, 0)),
            pl.BlockSpec((6, d), lambda i, j: (0, 0)),
            pl.BlockSpec((1, 6, d), lambda i, j: (i // tpb, 0, 0)),
            pl.BlockSpec((d, tn), lambda i, j: (0, j)),
        ],
        out_specs=pl.BlockSpec((tm, tn), lambda i, j: (i, j)),
        out_shape=jax.ShapeDtypeStruct((t, n), F32),
        scratch_shapes=[pltpu.VMEM((tm, d), BF16)],
        compiler_params=_cparams(("parallel", "arbitrary")),
        name="modmm",
    )(xs, g.reshape(1, d), mod_c, mod_l, w)


def _outproj_kernel(a1_ref, a2_ref, w1_ref, w2_ref, x_ref, mc_ref, ml_ref, o_ref, *, tiles_per_batch, g_idx):
    i = pl.program_id(0)
    acc = _dot(a1_ref[...], w1_ref[...]) + _dot(a2_ref[...], w2_ref[...])
    gate = _pick_mod(i, tiles_per_batch, acc.shape[0], mc_ref[g_idx:g_idx + 1, :], ml_ref[0, g_idx:g_idx + 1, :])
    o_ref[...] = x_ref[...] + gate * acc


def _outproj(a1, a2, w, xs, mod_c, mod_l, *, nt, g_idx):
    t, d = xs.shape
    k1 = a1.shape[1]
    k2 = a2.shape[1]
    tm, tn = 768, 1024
    tpb = nt // tm
    assert k1 == k2
    return pl.pallas_call(
        functools.partial(_outproj_kernel, tiles_per_batch=tpb, g_idx=g_idx),
        grid=(t // tm, d // tn),
        in_specs=[
            pl.BlockSpec((tm, k1), lambda i, j: (i, 0)),
            pl.BlockSpec((tm, k2), lambda i, j: (i, 0)),
            pl.BlockSpec((k1, tn), lambda i, j: (0, j)),
            pl.BlockSpec((k2, tn), lambda i, j: (1, j)),
            pl.BlockSpec((tm, tn), lambda i, j: (i, j)),
            pl.BlockSpec((6, tn), lambda i, j: (0, j)),
            pl.BlockSpec((1, 6, tn), lambda i, j: (i // tpb, 0, j)),
        ],
        out_specs=pl.BlockSpec((tm, tn), lambda i, j: (i, j)),
        out_shape=jax.ShapeDtypeStruct((t, d), F32),
        compiler_params=_cparams(("parallel", "parallel")),
        name="outproj",
    )(a1, a2, w, w, xs, mod_c, mod_l)


def _mla_prep_kernel(u_ref, lnq_ref, lnkv_ref, wqn_ref, wqr_ref, wqs_ref, wkk_ref, wkv_ref,
                     gq_ref, gk_ref, cos_ref, sin_ref, q_ref, k_ref, v_ref):
    u = u_ref[...]
    cq = u[:, :MLA_Q_RANK]
    ckv = u[:, MLA_Q_RANK:MLA_Q_RANK + MLA_KV_RANK]
    o = MLA_Q_RANK + MLA_KV_RANK
    kpe = u[:, o:o + MLA_ROPE]
    kpe_s = u[:, o + MLA_ROPE:o + 2 * MLA_ROPE]
    cqn = (_rms(cq) * lnq_ref[...]).astype(BF16)
    ckvn = (_rms(ckv) * lnkv_ref[...]).astype(BF16)
    qn = _dot(cqn, wqn_ref[...])
    qr = _dot(cqn, wqr_ref[...])
    qs = _dot(cqn, wqs_ref[...])
    kn = _dot(ckvn, wkk_ref[...])
    vv = _dot(ckvn, wkv_ref[...])
    cos = cos_ref[...]
    sin = sin_ref[...]
    gq = gq_ref[...]
    gk = gk_ref[...]
    gq_n, gq_r, gq_s = gq[:, :128], gq[:, 128:192], gq[:, 192:256]
    gk_n, gk_r, gk_s = gk[:, :128], gk[:, 128:192], gk[:, 192:256]
    kpe_rot = (kpe * gk_r) * cos + (kpe_s * gk_s) * sin
    kpe_sq = jnp.sum(kpe * kpe, axis=-1, keepdims=True)
    scale = MLA_QK ** -0.5
    for h in range(MLA_HEADS):
        qn_h = qn[:, h * 128:(h + 1) * 128]
        qr_h = qr[:, h * 64:(h + 1) * 64]
        qs_h = qs[:, h * 64:(h + 1) * 64]
        ms = (jnp.sum(qn_h * qn_h, axis=-1, keepdims=True) + jnp.sum(qr_h * qr_h, axis=-1, keepdims=True)) / MLA_QK
        r = lax.rsqrt(ms + EPS) * scale
        q_ref[0, h, :, 0:128] = (qn_h * gq_n * r).astype(BF16)
        q_ref[0, h, :, 128:192] = (((qr_h * gq_r) * cos + (qs_h * gq_s) * sin) * r).astype(BF16)
        kn_h = kn[:, h * 128:(h + 1) * 128]
        ms = (jnp.sum(kn_h * kn_h, axis=-1, keepdims=True) + kpe_sq) / MLA_QK
        r = lax.rsqrt(ms + EPS)
        k_ref[0, h, :, 0:128] = (kn_h * gk_n * r).astype(BF16)
        k_ref[0, h, :, 128:192] = (kpe_rot * r).astype(BF16)
        v_ref[0, h] = vv[:, h * 128:(h + 1) * 128].astype(BF16)


def _mla_prep(u, lnq, lnkv, wqn, wqr, wqs, wkk, wkv, gq, gk, cos, sin, *, batch, nt):
    tq = 256
    npb = nt // tq
    full = lambda a: pl.BlockSpec(a.shape, lambda b, i: (0,) * a.ndim)
    hm = lambda dd: pl.BlockSpec((1, MLA_HEADS, tq, dd), lambda b, i: (b, 0, i, 0))
    return pl.pallas_call(
        _mla_prep_kernel,
        grid=(batch, npb),
        in_specs=[
            pl.BlockSpec((tq, 1024), lambda b, i: (b * npb + i, EV_MLA // 1024)),
            full(lnq), full(lnkv), full(wqn), full(wqr), full(wqs), full(wkk), full(wkv), full(gq), full(gk),
            pl.BlockSpec((tq, MLA_ROPE), lambda b, i: (i, 0)),
            pl.BlockSpec((tq, MLA_ROPE), lambda b, i: (i, 0)),
        ],
        out_specs=[hm(MLA_QK), hm(MLA_QK), hm(MLA_V)],
        out_shape=[
            jax.ShapeDtypeStruct((batch, MLA_HEADS, nt, MLA_QK), BF16),
            jax.ShapeDtypeStruct((batch, MLA_HEADS, nt, MLA_QK), BF16),
            jax.ShapeDtypeStruct((batch, MLA_HEADS, nt, MLA_V), BF16),
        ],
        compiler_params=_cparams(("parallel", "parallel")),
        name="mla_prep",
    )(u, lnq, lnkv, wqn, wqr, wqs, wkk, wkv, gq, gk, cos, sin)


def _softmax_pv(s, v):
    m = jnp.max(s, axis=-1, keepdims=True)
    p = jnp.exp(s - m)
    l = jnp.sum(p, axis=-1, keepdims=True)
    return _dot(p.astype(BF16), v) / l


def _mla_attn_kernel(q_ref, k_ref, v_ref, o_ref, *, nt):
    qi = pl.program_id(2)
    q = q_ref[0, 0]

    def attend(nk):
        s = _dot_nt(q, k_ref[0, 0, :nk, :])
        o_ref[...] = _softmax_pv(s, v_ref[0, 0, :nk, :]).astype(BF16)

    @pl.when(qi == 0)
    def _():
        attend(CTX_LEN)

    @pl.when(qi > 0)
    def _():
        attend(nt)


def _mla_attn(q, k, v, *, batch, nt):
    tq = CTX_LEN
    npb = nt // tq
    return pl.pallas_call(
        functools.partial(_mla_attn_kernel, nt=nt),
        grid=(batch, MLA_HEADS, npb),
        in_specs=[
            pl.BlockSpec((1, 1, tq, MLA_QK), lambda b, h, i: (b, h, i, 0)),
            pl.BlockSpec((1, 1, nt, MLA_QK), lambda b, h, i: (b, h, 0, 0)),
            pl.BlockSpec((1, 1, nt, MLA_V), lambda b, h, i: (b, h, 0, 0)),
        ],
        out_specs=pl.BlockSpec((tq, MLA_V), lambda b, h, i: (b * npb + i, h)),
        out_shape=jax.ShapeDtypeStruct((batch * nt, MLA_HEADS * MLA_V), BF16),
        compiler_params=_cparams(("parallel", "parallel", "parallel")),
        name="mla_attn",
    )(q, k, v)


def _conv_kernel(x_ref, w_ref, b_ref, o_ref, *, nt):
    x = x_ref[...]
    row = lax.broadcasted_iota(jnp.int32, (nt, 1), 0)
    is_ctx = row < CTX_LEN
    p = jnp.where(is_ctx, row, row - CTX_LEN)
    n = jnp.where(is_ctx, CTX_LEN, nt - CTX_LEN)
    acc = b_ref[...] + x * w_ref[SSM_CONV // 2:SSM_CONV // 2 + 1, :]
    for j in range(SSM_CONV):
        off = j - SSM_CONV // 2
        if off == 0:
            continue
        shifted = pltpu.roll(x, (-off) % nt, 0)
        ok = jnp.logical_and(p + off >= 0, p + off < n)
        acc = acc + jnp.where(ok, shifted, 0.0) * w_ref[j:j + 1, :]
    o_ref[...] = acc * _sigmoid(acc)


def _conv(u, w, b, *, batch, nt):
    ct = 256
    c = w.shape[1]
    return pl.pallas_call(
        functools.partial(_conv_kernel, nt=nt),
        grid=(batch, c // ct),
        in_specs=[
            pl.BlockSpec((nt, ct), lambda bb, j: (bb, EV_XBC // ct + j)),
            pl.BlockSpec((SSM_CONV, ct), lambda bb, j: (0, j)),
            pl.BlockSpec((1, ct), lambda bb, j: (0, j)),
        ],
        out_specs=pl.BlockSpec((nt, ct), lambda bb, j: (bb, j)),
        out_shape=jax.ShapeDtypeStruct((batch * nt, c), F32),
        compiler_params=_cparams(("parallel", "parallel")),
        name="ssm_conv",
    )(u, w, b.reshape(1, c))


def _tri(reverse):
    r = lax.broadcasted_iota(jnp.int32, (CHUNK, CHUNK), 0)
    c = lax.broadcasted_iota(jnp.int32, (CHUNK, CHUNK), 1)
    return (c >= r) if reverse else (c <= r)


def _cumsums(a, mask):
    tri = mask.astype(F32)
    cs = jnp.dot(tri, a, precision=HIGHEST, preferred_element_type=F32)
    cs_t = lax.dot_general(a.T, tri, (((1,), (1,)), ((), ())), precision=HIGHEST, preferred_element_type=F32)
    return cs, cs_t


def _ssd_dir(xa, dtr, bias, a_neg, st_ref, y_ref, d, reverse):
    mask = _tri(reverse)
    xs = xa[:, :SSM_INNER]
    bm = xa[:, SSM_INNER:SSM_INNER + SSM_GROUPS * SSM_STATE]
    cm = xa[:, SSM_INNER + SSM_GROUPS * SSM_STATE:]
    dt = _softplus(dtr + bias)
    a = dt * a_neg
    cs, cs_t = _cumsums(a, mask)
    end = 0 if reverse else CHUNK - 1
    tot = cs[end:end + 1, :]
    rep = SSM_HEADS // SSM_GROUPS
    for g in range(SSM_GROUPS):
        bg = bm[:, g * SSM_STATE:(g + 1) * SSM_STATE]
        cg = cm[:, g * SSM_STATE:(g + 1) * SSM_STATE].astype(BF16)
        bg_t = bg.T.astype(BF16)
        cb = _dot(cg, bg_t)
        for r in range(rep):
            h = g * rep + r
            col = d * SSM_HEADS + h
            cs_col = cs[:, col:col + 1]
            cs_row = cs_t[col:col + 1, :]
            tot_h = tot[:, col:col + 1]
            lm = jnp.exp(jnp.where(mask, cs_col - cs_row, NEG))
            xg = xs[:, h * SSM_HEAD_DIM:(h + 1) * SSM_HEAD_DIM] * dt[:, col:col + 1]
            st = st_ref[h]
            y_d = _dot((cb * lm).astype(BF16), xg.astype(BF16))
            y_o = _dot(cg, st.astype(BF16)) * jnp.exp(cs_col)
            y_ref[:, h * SSM_HEAD_DIM:(h + 1) * SSM_HEAD_DIM] = y_d + y_o
            decay = jnp.exp(tot_h - cs_col)
            st_ref[h] = jnp.exp(tot_h) * st + _dot(bg_t, (xg * decay).astype(BF16))


def _ssd_kernel(xf_ref, xb_ref, dtf_ref, dtb_ref, bias_ref, alog_ref, yf_ref, yb_ref, stf_ref, stb_ref):
    @pl.when(pl.program_id(1) == 0)
    def _():
        stf_ref[...] = jnp.zeros_like(stf_ref)
        stb_ref[...] = jnp.zeros_like(stb_ref)

    bias = bias_ref[...]
    a_neg = -jnp.exp(alog_ref[...])
    _ssd_dir(xf_ref[...], dtf_ref[...], bias, a_neg, stf_ref, yf_ref, 0, False)
    _ssd_dir(xb_ref[...], dtb_ref[...], bias, a_neg, stb_ref, yb_ref, 1, True)


def _bwd_chunk(s, nc):
    ncc = CTX_LEN // CHUNK
    return jnp.where(s < ncc, ncc - 1 - s, nc + ncc - 1 - s)


def _ssd(xa, u, dt_bias, a_log, *, batch, nt):
    nc = nt // CHUNK
    c = xa.shape[1]
    dt_blk = EV_DT // LANE
    fwd = lambda b, s: (b * nc + s, 0)
    bwd = lambda b, s: (b * nc + _bwd_chunk(s, nc), 0)
    return pl.pallas_call(
        _ssd_kernel,
        grid=(batch, nc),
        in_specs=[
            pl.BlockSpec((CHUNK, c), fwd),
            pl.BlockSpec((CHUNK, c), bwd),
            pl.BlockSpec((CHUNK, LANE), lambda b, s: (b * nc + s, dt_blk)),
            pl.BlockSpec((CHUNK, LANE), lambda b, s: (b * nc + _bwd_chunk(s, nc), dt_blk)),
            pl.BlockSpec((1, LANE), lambda b, s: (0, 0)),
            pl.BlockSpec((1, LANE), lambda b, s: (0, 0)),
        ],
        out_specs=[pl.BlockSpec((CHUNK, SSM_INNER), fwd), pl.BlockSpec((CHUNK, SSM_INNER), bwd)],
        out_shape=[jax.ShapeDtypeStruct((batch * nt, SSM_INNER), F32)] * 2,
        scratch_shapes=[pltpu.VMEM((SSM_HEADS, SSM_STATE, SSM_HEAD_DIM), F32)] * 2,
        compiler_params=_cparams(("parallel", "arbitrary")),
        name="ssd_scan",
    )(xa, xa, u, u, dt_bias, a_log)


def _ssm_out_kernel(yf_ref, yb_ref, xs_ref, z_ref, dsk_ref, g_ref, o_ref):
    z = z_ref[...]
    y = yf_ref[...] + yb_ref[...] + dsk_ref[...] * xs_ref[...]
    y = y * (z * _sigmoid(z))
    o_ref[...] = (_rms(y) * g_ref[...]).astype(BF16)


def _ssm_out(yf, yb, xa, u, d_skip, g):
    t = yf.shape[0]
    tm = 256
    blk = pl.BlockSpec((tm, SSM_INNER), lambda i: (i, 0))
    vec = pl.BlockSpec((1, SSM_INNER), lambda i: (0, 0))
    return pl.pallas_call(
        _ssm_out_kernel,
        grid=(t // tm,),
        in_specs=[blk, blk, blk, blk, vec, vec],
        out_specs=blk,
        out_shape=jax.ShapeDtypeStruct((t, SSM_INNER), BF16),
        compiler_params=_cparams(("parallel",)),
        name="ssm_out",
    )(yf, yb, xa, u, d_skip, g)


def _na_kernel(q_ref, k_ref, v_ref, gq_ref, gk_ref, bias_ref, o_ref, qs, ks, vs, *, nt):
    scale = NA_HEAD_DIM ** -0.5
    qs[...] = (_rms(q_ref[...]) * (gq_ref[...] * scale)).astype(BF16)
    ks[...] = (_rms(k_ref[...]) * gk_ref[...]).astype(BF16)
    vs[...] = v_ref[...].astype(BF16)
    kc = ks[0:CTX_LEN, :]
    vc = vs[0:CTX_LEN, :]
    o_ref[0:CTX_LEN, :] = _softmax_pv(_dot_nt(qs[0:CTX_LEN, :], kc), vc).astype(BF16)
    rows = (nt - CTX_LEN) // GRID_W
    band = NA_ROWS * GRID_W

    def body(r, carry):
        r0 = jnp.clip(r - NA_ROWS // 2, 0, rows - NA_ROWS)
        q = qs[pl.ds(pl.multiple_of(CTX_LEN + r * GRID_W, GRID_W), GRID_W), :]
        start = pl.multiple_of(CTX_LEN + r0 * GRID_W, GRID_W)
        kb = ks[pl.ds(start, band), :]
        vb = vs[pl.ds(start, band), :]
        s_loc = _dot_nt(q, kb) + bias_ref[0, r - r0]
        s_ctx = _dot_nt(q, kc)
        m = jnp.maximum(jnp.max(s_loc, axis=-1, keepdims=True), jnp.max(s_ctx, axis=-1, keepdims=True))
        p_loc = jnp.exp(s_loc - m)
        p_ctx = jnp.exp(s_ctx - m)
        l = jnp.sum(p_loc, axis=-1, keepdims=True) + jnp.sum(p_ctx, axis=-1, keepdims=True)
        o = (_dot(p_loc.astype(BF16), vb) + _dot(p_ctx.astype(BF16), vc)) / l
        o_ref[pl.ds(pl.multiple_of(CTX_LEN + r * GRID_W, GRID_W), GRID_W), :] = o.astype(BF16)
        return carry

    lax.fori_loop(0, rows, body, 0)


def _na(u, gq, gk, bias, *, batch, nt):
    hd = NA_HEAD_DIM
    band = NA_ROWS * GRID_W
    return pl.pallas_call(
        functools.partial(_na_kernel, nt=nt),
        grid=(batch, NA_HEADS),
        in_specs=[
            pl.BlockSpec((nt, hd), lambda b, h: (b, h)),
            pl.BlockSpec((nt, hd), lambda b, h: (b, NA_HEADS + h)),
            pl.BlockSpec((nt, hd), lambda b, h: (b, 2 * NA_HEADS + h)),
            pl.BlockSpec((1, hd), lambda b, h: (0, 0)),
            pl.BlockSpec((1, hd), lambda b, h: (0, 0)),
            pl.BlockSpec((1, NA_ROWS, GRID_W, band), lambda b, h: (h, 0, 0, 0)),
        ],
        out_specs=pl.BlockSpec((nt, hd), lambda b, h: (b, h)),
        out_shape=jax.ShapeDtypeStruct((batch * nt, NA_WIDTH), BF16),
        scratch_shapes=[pltpu.VMEM((nt, hd), BF16)] * 3,
        compiler_params=_cparams(("parallel", "parallel")),
        name="na_attn",
    )(u, u, u, gq, gk, bias)


def _na_bias_table(rpb):
    col = np.arange(GRID_W)
    col_start = np.clip(col - NA_COLS // 2, 0, GRID_W - NA_COLS)
    col_ok = (col[None, :] >= col_start[:, None]) & (col[None, :] < col_start[:, None] + NA_COLS)
    dc = np.clip(col[None, :] - col[:, None], -(NA_COLS - 1), NA_COLS - 1) + NA_COLS - 1
    var = np.arange(NA_ROWS)
    kr = np.arange(NA_ROWS)
    dr = kr[None, :] - var[:, None] + NA_ROWS - 1
    b = rpb[:, dr][:, :, :, dc]
    b = jnp.where(jnp.asarray(col_ok)[None, None, None], b, NEG)
    b = b.transpose(0, 1, 3, 2, 4)
    return b.reshape(NA_HEADS, NA_ROWS, GRID_W, NA_ROWS * GRID_W).astype(F32)


def _mlstm_dir(q, k, v, gm, gb, ct_ref, n_ref, m_ref, h_ref, d, reverse):
    mask = _tri(reverse)
    g_all = gm + gb
    lf_all = jnp.minimum(g_all, 0.0) - jnp.log(1.0 + jnp.exp(-jnp.abs(g_all)))
    b_all, b_all_t = _cumsums(lf_all, mask)
    g_all_t = g_all.T
    end = 0 if reverse else CHUNK - 1
    for h in range(ML_HEADS):
        icol = d * ML_HEADS + h
        fcol = 2 * ML_HEADS + d * ML_HEADS + h
        b_col = b_all[:, fcol:fcol + 1]
        b_row = b_all_t[fcol:fcol + 1, :]
        li_col = g_all[:, icol:icol + 1]
        li_row = g_all_t[icol:icol + 1, :]
        m_prev = m_ref[h][0:1, 0:1]
        dmat = jnp.where(mask, b_col - b_row + li_row, NEG)
        inter = b_col + m_prev
        m_t = jnp.maximum(inter, jnp.max(dmat, axis=1, keepdims=True))
        qh = q[:, h * ML_QK:(h + 1) * ML_QK]
        kh = k[:, h * ML_QK:(h + 1) * ML_QK] * (ML_QK ** -0.5)
        vh = v[:, h * ML_V:(h + 1) * ML_V]
        qb = qh.astype(BF16)
        kb = kh.astype(BF16)
        s = _dot_nt(qb, kb) * jnp.exp(dmat - m_t)
        w_inter = jnp.exp(inter - m_t)
        ct = ct_ref[h]
        nvec = n_ref[h][0:1, :]
        num = _dot(s.astype(BF16), vh.astype(BF16)) + w_inter * _dot(qb, ct.astype(BF16))
        den = jnp.sum(s, axis=1, keepdims=True) + w_inter * jnp.sum(qh * nvec, axis=1, keepdims=True)
        h_ref[:, h * ML_V:(h + 1) * ML_V] = num / jnp.maximum(jnp.abs(den), jnp.exp(-m_t))
        b_end = b_col[end:end + 1, :]
        g_col = b_end - b_col + li_col
        m_new = jnp.maximum(b_end + m_prev, jnp.max(g_col, axis=0, keepdims=True))
        wk = jnp.exp(g_col - m_new)
        decay = jnp.exp(b_end + m_prev - m_new)
        ct_ref[h] = decay * ct + _dot(kh.T.astype(BF16), (wk * vh).astype(BF16))
        n_ref[h] = jnp.broadcast_to(decay * nvec + jnp.sum(wk * kh, axis=0, keepdims=True), (8, ML_QK))
        m_ref[h] = jnp.broadcast_to(m_new, (8, LANE))


def _mlstm_kernel(qf_ref, kf_ref, vf_ref, gf_ref, qb_ref, kb_ref, vb_ref, gbk_ref, gb_ref,
                  hf_ref, hb_ref, ctf, nf, mf, ctb, nb, mb):
    @pl.when(pl.program_id(1) == 0)
    def _():
        for r in (ctf, nf, mf, ctb, nb, mb):
            r[...] = jnp.zeros_like(r)

    gb = gb_ref[...]
    _mlstm_dir(qf_ref[...], kf_ref[...], vf_ref[...], gf_ref[...], gb, ctf, nf, mf, hf_ref, 0, False)
    _mlstm_dir(qb_ref[...], kb_ref[...], vb_ref[...], gbk_ref[...], gb, ctb, nb, mb, hb_ref, 1, True)


def _mlstm(u, gate_b, *, batch, nt):
    nc = nt // CHUNK
    wqk = ML_HEADS * ML_QK
    wv = ML_HEADS * ML_V
    qo, ko, vo, go = 3 * NA_WIDTH, 3 * NA_WIDTH + wqk, 3 * NA_WIDTH + 2 * wqk, OD_GATE

    def specs(chunk_of):
        return [
            pl.BlockSpec((CHUNK, wqk), lambda b, s: (b * nc + chunk_of(s), qo // wqk)),
            pl.BlockSpec((CHUNK, wqk), lambda b, s: (b * nc + chunk_of(s), ko // wqk)),
            pl.BlockSpec((CHUNK, wv), lambda b, s: (b * nc + chunk_of(s), vo // wv)),
            pl.BlockSpec((CHUNK, LANE), lambda b, s: (b * nc + chunk_of(s), go // LANE)),
        ]

    fwd_of = lambda s: s
    bwd_of = lambda s: _bwd_chunk(s, nc)
    state = [pltpu.VMEM((ML_HEADS, ML_QK, ML_V), F32), pltpu.VMEM((ML_HEADS, 8, ML_QK), F32),
             pltpu.VMEM((ML_HEADS, 8, LANE), F32)]
    return pl.pallas_call(
        _mlstm_kernel,
        grid=(batch, nc),
        in_specs=specs(fwd_of) + specs(bwd_of) + [pl.BlockSpec((1, LANE), lambda b, s: (0, 0))],
        out_specs=[pl.BlockSpec((CHUNK, wv), lambda b, s: (b * nc + s, 0)),
                   pl.BlockSpec((CHUNK, wv), lambda b, s: (b * nc + _bwd_chunk(s, nc), 0))],
        out_shape=[jax.ShapeDtypeStruct((batch * nt, wv), F32)] * 2,
        scratch_shapes=state + state,
        compiler_params=_cparams(("parallel", "arbitrary")),
        name="mlstm_scan",
    )(u, u, u, u, u, u, u, u, gate_b)


def _mlstm_out_kernel(hf_ref, hb_ref, o_ref_in, g_ref, o_ref):
    hs = hf_ref[...] + hb_ref[...]
    og = _sigmoid(o_ref_in[...])
    g = g_ref[...]
    for h in range(ML_HEADS):
        sl = slice(h * ML_V, (h + 1) * ML_V)
        o_ref[:, sl] = (_rms(hs[:, sl]) * g[:, sl] * og[:, sl]).astype(BF16)


def _mlstm_out(hf, hb, u, g):
    t = hf.shape[0]
    tm = 256
    wv = ML_HEADS * ML_V
    blk = pl.BlockSpec((tm, wv), lambda i: (i, 0))
    o_blk = (3 * NA_WIDTH + 2 * ML_HEADS * ML_QK + wv) // wv
    return pl.pallas_call(
        _mlstm_out_kernel,
        grid=(t // tm,),
        in_specs=[blk, blk, pl.BlockSpec((tm, wv), lambda i: (i, o_blk)), pl.BlockSpec((1, wv), lambda i: (0, 0))],
        out_specs=blk,
        out_shape=jax.ShapeDtypeStruct((t, wv), BF16),
        compiler_params=_cparams(("parallel",)),
        name="mlstm_out",
    )(hf, hb, u, g)


def _router_kernel(x_ref, g_ref, mc_ref, ml_ref, rw_ref, rb_ref, h_ref, ti_ref, tg_ref, *, tiles_per_batch, sh_idx, sc_idx):
    i = pl.program_id(0)
    h = _modulated(x_ref[...], g_ref[...], i, tiles_per_batch, mc_ref, ml_ref, sh_idx, sc_idx)
    h_ref[...] = h
    lg = jnp.dot(h, rw_ref[...], precision=HIGHEST, preferred_element_type=F32) + rb_ref[...]
    tm = lg.shape[0]
    lane = lax.broadcasted_iota(jnp.int32, (tm, LANE), 1)
    lg = jnp.where(lane < N_EXPERTS, lg, NEG)
    vals, idxs = [], []
    for _ in range(TOP_K):
        m = jnp.max(lg, axis=1, keepdims=True)
        idx = jnp.min(jnp.where(lg == m, lane, LANE), axis=1, keepdims=True)
        vals.append(m)
        idxs.append(idx)
        lg = jnp.where(lane == idx, 2.0 * NEG, lg)
    es = [jnp.exp(v - vals[0]) for v in vals]
    den = es[0] + es[1] + es[2] + es[3]
    ti = jnp.zeros((tm, LANE), jnp.int32)
    tg = jnp.zeros((tm, LANE), F32)
    for kk in range(TOP_K):
        ti = jnp.where(lane == kk, idxs[kk], ti)
        tg = jnp.where(lane == kk, es[kk] / den, tg)
    ti_ref[...] = ti
    tg_ref[...] = tg


def _router(xs, g, mod_c, mod_l, rw, rb, *, nt, sh_idx, sc_idx):
    t, d = xs.shape
    tm = 768
    tpb = nt // tm
    return pl.pallas_call(
        functools.partial(_router_kernel, tiles_per_batch=tpb, sh_idx=sh_idx, sc_idx=sc_idx),
        grid=(t // tm,),
        in_specs=[
            pl.BlockSpec((tm, d), lambda i: (i, 0)),
            pl.BlockSpec((1, d), lambda i: (0, 0)),
            pl.BlockSpec((6, d), lambda i: (0, 0)),
            pl.BlockSpec((1, 6, d), lambda i: (i // tpb, 0, 0)),
            pl.BlockSpec((d, LANE), lambda i: (0, 0)),
            pl.BlockSpec((1, LANE), lambda i: (0, 0)),
        ],
        out_specs=[pl.BlockSpec((tm, d), lambda i: (i, 0)),
                   pl.BlockSpec((tm, LANE), lambda i: (i, 0)),
                   pl.BlockSpec((tm, LANE), lambda i: (i, 0))],
        out_shape=[jax.ShapeDtypeStruct((t, d), F32),
                   jax.ShapeDtypeStruct((t, LANE), jnp.int32),
                   jax.ShapeDtypeStruct((t, LANE), F32)],
        compiler_params=_cparams(("parallel",)),
        name="router",
    )(xs, g.reshape(1, d), mod_c, mod_l, rw, rb)


EXPERT_TM = 256
DISPATCH_TM = 256
COMBINE_TM = 128


def _dispatch_kernel(pos_ref, ztile_ref, h_ref, xs_hbm, zbuf, sem, zsem):
    i = pl.program_id(0)

    @pl.when(i == 0)
    def _():
        zbuf[...] = jnp.zeros_like(zbuf)

        def zcopy(e):
            return pltpu.make_async_copy(zbuf, xs_hbm.at[pl.ds(ztile_ref[e] * EXPERT_TM, EXPERT_TM)], zsem)

        def start(e, c):
            @pl.when(ztile_ref[e] >= 0)
            def _():
                zcopy(e).start()
            return c

        def wait(e, c):
            @pl.when(ztile_ref[e] >= 0)
            def _():
                zcopy(e).wait()
            return c

        lax.fori_loop(0, N_EXPERTS, start, 0)
        lax.fori_loop(0, N_EXPERTS, wait, 0)

    def row_copy(r, kk):
        p = pos_ref[0, 0, r * TOP_K + kk]
        return pltpu.make_async_copy(h_ref.at[pl.ds(r, 1)], xs_hbm.at[pl.ds(p, 1)], sem)

    def start(r, c):
        for kk in range(TOP_K):
            row_copy(r, kk).start()
        return c

    def wait(r, c):
        for kk in range(TOP_K):
            row_copy(r, kk).wait()
        return c

    lax.fori_loop(0, DISPATCH_TM, start, 0)
    lax.fori_loop(0, DISPATCH_TM, wait, 0)


def _dispatch(h, pos, ztile, *, n_rows):
    t, d = h.shape
    tm = DISPATCH_TM
    nsteps = t // tm
    return pl.pallas_call(
        _dispatch_kernel,
        grid=(nsteps,),
        in_specs=[
            pl.BlockSpec((1, 1, tm * TOP_K), lambda i: (i, 0, 0), memory_space=pltpu.SMEM),
            pl.BlockSpec(memory_space=pltpu.SMEM),
            pl.BlockSpec((tm, d), lambda i: (i, 0)),
        ],
        out_specs=pl.BlockSpec(memory_space=pl.ANY),
        out_shape=jax.ShapeDtypeStruct((n_rows, d), F32),
        scratch_shapes=[pltpu.VMEM((EXPERT_TM, d), F32), pltpu.SemaphoreType.DMA, pltpu.SemaphoreType.DMA],
        compiler_params=_cparams(("arbitrary",)),
        name="moe_dispatch",
    )(pos.reshape(nsteps, 1, tm * TOP_K), ztile, h)


def _expert_kernel(te_ref, nu_ref, x_ref, w1_ref, b1_ref, w2_ref, b2_ref, perm_ref, y_ref, w1b, w2b):
    i = pl.program_id(0)
    live = i < nu_ref[0]
    changed = jnp.logical_or(i == 0, te_ref[i] != te_ref[jnp.maximum(i - 1, 0)])

    @pl.when(jnp.logical_and(live, changed))
    def _():
        w1b[...] = w1_ref[0, 0].astype(BF16)
        w2b[...] = w2_ref[0, 0].astype(BF16)

    @pl.when(live)
    def _():
        u = _dot(x_ref[...].astype(BF16), w1b[...]) + b1_ref[0, 0]
        ud = _dot(u.astype(BF16), perm_ref[...])
        glu = jnp.minimum(ud[:, :D_EXPERT], SWIGLU_LIMIT)
        lin = jnp.clip(ud[:, D_EXPERT:], -SWIGLU_LIMIT, SWIGLU_LIMIT)
        act = glu * _sigmoid(SWIGLU_ALPHA * glu) * (lin + 1.0)
        y_ref[...] = _dot(act.astype(BF16), w2b[...]) + b2_ref[0, 0]


def _experts(xsorted, tile_expert, n_used, w1, b1, w2, b2, perm, *, layer):
    p, d = xsorted.shape
    tm = EXPERT_TM
    ntiles = p // tm
    de2 = 2 * D_EXPERT
    tile = lambda i, te, nu: (jnp.minimum(i, nu[0] - 1), 0)
    grid_spec = pltpu.PrefetchScalarGridSpec(
        num_scalar_prefetch=2,
        grid=(ntiles,),
        in_specs=[
            pl.BlockSpec((tm, d), tile),
            pl.BlockSpec((1, 1, d, de2), lambda i, te, nu: (layer, te[i], 0, 0)),
            pl.BlockSpec((1, 1, 1, de2), lambda i, te, nu: (layer, te[i], 0, 0)),
            pl.BlockSpec((1, 1, D_EXPERT, d), lambda i, te, nu: (layer, te[i], 0, 0)),
            pl.BlockSpec((1, 1, 1, d), lambda i, te, nu: (layer, te[i], 0, 0)),
            pl.BlockSpec((de2, de2), lambda i, te, nu: (0, 0)),
        ],
        out_specs=pl.BlockSpec((tm, d), tile),
        scratch_shapes=[pltpu.VMEM((d, de2), BF16), pltpu.VMEM((D_EXPERT, d), BF16)],
    )
    return pl.pallas_call(
        _expert_kernel,
        grid_spec=grid_spec,
        out_shape=jax.ShapeDtypeStruct((p, d), F32),
        compiler_params=_cparams(("arbitrary",)),
        name="moe_experts",
    )(tile_expert, n_used, xsorted, w1, b1.reshape(DEPTH, N_EXPERTS, 1, de2), w2, b2.reshape(DEPTH, N_EXPERTS, 1, d), perm)


def _combine_kernel(pos_ref, posn_ref, x_ref, tg_ref, mc_ref, ml_ref, y_hbm, o_ref, buf, sem, *, tiles_per_batch, g_idx):
    i = pl.program_id(0)
    n = pl.num_programs(0)
    tm = COMBINE_TM
    slot = i % 2

    def issue(pref, sl):
        def body(r, c):
            for kk in range(TOP_K):
                p = pref[0, 0, r * TOP_K + kk]
                pltpu.make_async_copy(y_hbm.at[pl.ds(p, 1)], buf.at[sl, kk, pl.ds(r, 1)], sem.at[sl]).start()
            return c
        lax.fori_loop(0, tm, body, 0)

    @pl.when(i == 0)
    def _():
        issue(pos_ref, 0)

    @pl.when(i + 1 < n)
    def _():
        issue(posn_ref, 1 - slot)

    for kk in range(TOP_K):
        pltpu.make_async_copy(y_hbm.at[pl.ds(0, tm)], buf.at[slot, kk], sem.at[slot]).wait()
    tg = tg_ref[...]
    acc = buf[slot, 0] * tg[:, 0:1]
    for kk in range(1, TOP_K):
        acc = acc + buf[slot, kk] * tg[:, kk:kk + 1]
    row = lax.broadcasted_iota(jnp.int32, (tm, 1), 0) + (i % tiles_per_batch) * tm
    gate = jnp.where(row < CTX_LEN, mc_ref[g_idx:g_idx + 1, :], ml_ref[0, g_idx:g_idx + 1, :])
    o_ref[...] = x_ref[...] + gate * acc


def _combine(xs, tg, pos, ybuf, mod_c, mod_l, *, nt, g_idx):
    t, d = xs.shape
    tm = COMBINE_TM
    nsteps = t // tm
    tpb = nt // tm
    pos3 = pos.reshape(nsteps, 1, tm * TOP_K)
    return pl.pallas_call(
        functools.partial(_combine_kernel, tiles_per_batch=tpb, g_idx=g_idx),
        grid=(nsteps,),
        in_specs=[
            pl.BlockSpec((1, 1, tm * TOP_K), lambda i: (i, 0, 0), memory_space=pltpu.SMEM),
            pl.BlockSpec((1, 1, tm * TOP_K), lambda i: (jnp.minimum(i + 1, nsteps - 1), 0, 0), memory_space=pltpu.SMEM),
            pl.BlockSpec((tm, d), lambda i: (i, 0)),
            pl.BlockSpec((tm, LANE), lambda i: (i, 0)),
            pl.BlockSpec((6, d), lambda i: (0, 0)),
            pl.BlockSpec((1, 6, d), lambda i: (i // tpb, 0, 0)),
            pl.BlockSpec(memory_space=pl.ANY),
        ],
        out_specs=pl.BlockSpec((tm, d), lambda i: (i, 0)),
        out_shape=jax.ShapeDtypeStruct((t, d), F32),
        scratch_shapes=[pltpu.VMEM((2, TOP_K, tm, d), F32), pltpu.SemaphoreType.DMA((2,))],
        compiler_params=_cparams(("arbitrary",)),
        name="moe_combine",
    )(pos3, pos3, xs, tg, mod_c, mod_l, ybuf)


def _routing_tables(ti, n_tiles):
    top_i = ti[:, :TOP_K]
    t = top_i.shape[0]
    cnt = jnp.sum(jax.nn.one_hot(top_i, N_EXPERTS, dtype=jnp.int32), axis=1)
    incl = jnp.cumsum(cnt, axis=0)
    rank = incl - cnt
    size = incl[-1]
    tiles = (size + EXPERT_TM - 1) // EXPERT_TM
    tile_end = jnp.cumsum(tiles)
    tile_start = tile_end - tiles
    pos = tile_start[top_i] * EXPERT_TM + jnp.take_along_axis(rank, top_i, axis=1)
    n_used = tile_end[-1]
    tid = jnp.minimum(jnp.arange(n_tiles, dtype=jnp.int32), n_used - 1)
    tile_expert = jnp.searchsorted(tile_end, tid, side="right").astype(jnp.int32)
    ztile = jnp.where(tiles > 0, tile_end - 1, -1).astype(jnp.int32)
    return pos.reshape(t * TOP_K).astype(jnp.int32), tile_expert, n_used.reshape(1).astype(jnp.int32), ztile


def _swiglu_perm():
    src = np.concatenate([np.arange(0, 2 * D_EXPERT, 2), np.arange(1, 2 * D_EXPERT, 2)])
    p = np.zeros((2 * D_EXPERT, 2 * D_EXPERT), np.float32)
    p[src, np.arange(2 * D_EXPERT)] = 1.0
    return jnp.asarray(p, dtype=BF16)


def _moe_layer(xs, l, norm_g, mod_c, mod_l, router_w, router_b, moe_w1, moe_b1, moe_w2, moe_b2, perm, *, nt):
    t, d = xs.shape
    rw = jnp.pad(router_w[l], ((0, 0), (0, LANE - N_EXPERTS)))
    rb = jnp.pad(router_b[l], (0, LANE - N_EXPERTS)).reshape(1, LANE)
    h, ti, tg = _router(xs, norm_g, mod_c, mod_l, rw, rb, nt=nt, sh_idx=3, sc_idx=4)
    n_tiles = t * TOP_K // EXPERT_TM + N_EXPERTS
    pos, tile_expert, n_used, ztile = _routing_tables(ti, n_tiles)
    xsorted = _dispatch(h, pos, ztile, n_rows=n_tiles * EXPERT_TM)
    ybuf = _experts(xsorted, tile_expert, n_used, moe_w1, moe_b1, moe_w2, moe_b2, perm, layer=l)
    return _combine(xs, tg, pos, ybuf, mod_c, mod_l, nt=nt, g_idx=5)


_ROPE_SWAP = np.concatenate([np.arange(16, 32), np.arange(0, 16), np.arange(48, 64), np.arange(32, 48)])


def _rope_tables(n_lat):
    pos = jnp.arange(n_lat)
    row = (pos // GRID_W).astype(F32)
    col = (pos % GRID_W).astype(F32)
    axis_dim = MLA_ROPE // 2
    inv_freq = ROPE_BASE ** (-jnp.arange(0, axis_dim, 2, dtype=F32) / axis_dim)
    ar, ac = row[:, None] * inv_freq, col[:, None] * inv_freq
    cos = jnp.concatenate([jnp.cos(ar), jnp.cos(ar), jnp.cos(ac), jnp.cos(ac)], axis=-1)
    sin = jnp.concatenate([-jnp.sin(ar), jnp.sin(ar), -jnp.sin(ac), jnp.sin(ac)], axis=-1)
    cos = jnp.concatenate([jnp.ones((CTX_LEN, MLA_ROPE), F32), cos], axis=0)
    sin = jnp.concatenate([jnp.zeros((CTX_LEN, MLA_ROPE), F32), sin], axis=0)
    return cos, sin


def _ev_w_in_layout(w):
    d = w.shape[0]
    o = 0
    mla = w[:, o:o + MLA_Q_RANK + MLA_KV_RANK + MLA_ROPE]; o += MLA_Q_RANK + MLA_KV_RANK + MLA_ROPE
    z = w[:, o:o + SSM_INNER]; o += SSM_INNER
    xbc = w[:, o:o + SSM_XBC]; o += SSM_XBC
    dt = w[:, o:o + 2 * SSM_HEADS]
    kpe = mla[:, MLA_Q_RANK + MLA_KV_RANK:]
    pad = jnp.zeros((d, EV_PAD_COLS - EV_DT - 2 * SSM_HEADS), w.dtype)
    return jnp.concatenate([z, xbc, mla, kpe[:, _ROPE_SWAP], dt, pad], axis=1).astype(BF16)


def _pad_lane(v):
    return jnp.pad(v.reshape(-1), (0, LANE - v.size)).reshape(1, LANE).astype(F32)


def _even_mixer(xs, l, i, norm_g, mod_c, mod_l, ev_w_in, ev_w_out, mla_ln_q, mla_ln_kv, mla_w_uq, mla_w_ukv,
                mla_q_norm, mla_k_norm, ssm_conv_w, ssm_conv_b, ssm_a_log, ssm_dt_bias, ssm_d, ssm_norm,
                cos, sin, *, batch, nt):
    u = _modmm(xs, norm_g, mod_c, mod_l, _ev_w_in_layout(ev_w_in[i]), nt=nt, sh_idx=0, sc_idx=1, tn=1024)
    wq = mla_w_uq[i].reshape(MLA_Q_RANK, MLA_HEADS, MLA_QK)
    wqn = wq[:, :, :MLA_NOPE].reshape(MLA_Q_RANK, -1).astype(BF16)
    wqr = wq[:, :, MLA_NOPE:]
    wqs = wqr[:, :, _ROPE_SWAP].reshape(MLA_Q_RANK, -1).astype(BF16)
    wqr = wqr.reshape(MLA_Q_RANK, -1).astype(BF16)
    wkv = mla_w_ukv[i].reshape(MLA_KV_RANK, MLA_HEADS, MLA_NOPE + MLA_V)
    wkk = wkv[:, :, :MLA_NOPE].reshape(MLA_KV_RANK, -1).astype(BF16)
    wvv = wkv[:, :, MLA_NOPE:].reshape(MLA_KV_RANK, -1).astype(BF16)
    gq = mla_q_norm[i]
    gk = mla_k_norm[i]
    gq3 = jnp.concatenate([gq, gq[MLA_NOPE:][_ROPE_SWAP]]).reshape(1, -1)
    gk3 = jnp.concatenate([gk, gk[MLA_NOPE:][_ROPE_SWAP]]).reshape(1, -1)
    q, k, v = _mla_prep(u, mla_ln_q[i].reshape(1, -1), mla_ln_kv[i].reshape(1, -1), wqn, wqr, wqs, wkk, wvv,
                        gq3, gk3, cos, sin, batch=batch, nt=nt)
    o_mla = _mla_attn(q, k, v, batch=batch, nt=nt)
    xa = _conv(u, ssm_conv_w[i], ssm_conv_b[i], batch=batch, nt=nt)
    yf, yb = _ssd(xa, u, _pad_lane(ssm_dt_bias[i]), _pad_lane(ssm_a_log[i]), batch=batch, nt=nt)
    d_skip = jnp.repeat(ssm_d[i], SSM_HEAD_DIM).reshape(1, SSM_INNER)
    o_ssm = _ssm_out(yf, yb, xa, u, d_skip, ssm_norm[i].reshape(1, SSM_INNER))
    return _outproj(o_mla, o_ssm, ev_w_out[i].astype(BF16), xs, mod_c, mod_l, nt=nt, g_idx=2)


def _odd_mixer(xs, l, i, norm_g, mod_c, mod_l, od_w_in, od_w_out, na_q_norm, na_k_norm, na_rpb, ml_gate_b, ml_norm,
               *, batch, nt):
    w = od_w_in[i]
    w = jnp.concatenate([w, jnp.zeros((w.shape[0], OD_PAD_COLS - w.shape[1]), w.dtype)], axis=1).astype(BF16)
    u = _modmm(xs, norm_g, mod_c, mod_l, w, nt=nt, sh_idx=0, sc_idx=1, tn=896)
    o_na = _na(u, na_q_norm[i].reshape(1, -1), na_k_norm[i].reshape(1, -1), _na_bias_table(na_rpb[i]),
               batch=batch, nt=nt)
    hf, hb = _mlstm(u, _pad_lane(ml_gate_b[i]), batch=batch, nt=nt)
    o_ml = _mlstm_out(hf, hb, u, ml_norm[i].reshape(1, -1))
    return _outproj(o_na, o_ml, od_w_out[i].astype(BF16), xs, mod_c, mod_l, nt=nt, g_idx=2)


def kernel(x, c, ctx, c_ctx, mod_w, mod_b, norm_mix, norm_ffn, ev_w_in, ev_w_out, mla_ln_q, mla_ln_kv, mla_w_uq,
           mla_w_ukv, mla_q_norm, mla_k_norm, ssm_conv_w, ssm_conv_b, ssm_a_log, ssm_dt_bias, ssm_d, ssm_norm,
           od_w_in, od_w_out, na_q_norm, na_k_norm, na_rpb, ml_gate_b, ml_norm, router_w, router_b, moe_w1, moe_b1,
           moe_w2, moe_b2):
    batch, seq, d = x.shape
    n_ctx = ctx.shape[1]
    assert n_ctx == CTX_LEN and d == D_MODEL and seq % GRID_W == 0
    nt = n_ctx + seq
    xs = jnp.concatenate([ctx, x], axis=1).reshape(batch * nt, d)
    cond = jnp.concatenate([c, c_ctx[None, :], jnp.zeros((8 - batch - 1, d), F32)], axis=0)
    mods = _mods(cond, mod_w, mod_b).reshape(DEPTH, 8, 6, d)
    cos, sin = _rope_tables(seq)
    perm = _swiglu_perm()
    for l in range(DEPTH):
        i = l // 2
        mod_l = mods[l, :batch]
        mod_c = mods[l, batch]
        if l % 2 == 0:
            xs = _even_mixer(xs, l, i, norm_mix[l], mod_c, mod_l, ev_w_in, ev_w_out, mla_ln_q, mla_ln_kv, mla_w_uq,
                             mla_w_ukv, mla_q_norm, mla_k_norm, ssm_conv_w, ssm_conv_b, ssm_a_log, ssm_dt_bias,
                             ssm_d, ssm_norm, cos, sin, batch=batch, nt=nt)
        else:
            xs = _odd_mixer(xs, l, i, norm_mix[l], mod_c, mod_l, od_w_in, od_w_out, na_q_norm, na_k_norm, na_rpb,
                            ml_gate_b, ml_norm, batch=batch, nt=nt)
        xs = _moe_layer(xs, l, norm_ffn[l], mod_c, mod_l, router_w, router_b, moe_w1, moe_b1, moe_w2, moe_b2, perm,
                        nt=nt)
    return xs.reshape(batch, nt, d)[:, n_ctx:, :]
```

```python
import functools
import math

import numpy as np
import jax
import jax.numpy as jnp
from jax import lax
from jax.experimental import pallas as pl
from jax.experimental.pallas import tpu as pltpu

F32 = jnp.float32
BF16 = jnp.bfloat16
HIGHEST = lax.Precision.HIGHEST

D_MODEL = 2048
DEPTH = 4
GRID_W = 64
CTX_LEN = 256
EPS = 1e-6
NEG = -1e30

MLA_HEADS = 8
MLA_NOPE = 128
MLA_ROPE = 64
MLA_V = 128
MLA_Q_RANK = 512
MLA_KV_RANK = 256
MLA_QK = MLA_NOPE + MLA_ROPE
ROPE_BASE = 10000.0

SSM_HEADS = 16
SSM_HEAD_DIM = 64
SSM_INNER = SSM_HEADS * SSM_HEAD_DIM
SSM_GROUPS = 4
SSM_STATE = 128
SSM_CONV = 5
SSM_XBC = SSM_INNER + 2 * SSM_GROUPS * SSM_STATE
CHUNK = 128

NA_HEADS = 8
NA_HEAD_DIM = 128
NA_ROWS = 8
NA_COLS = 16
NA_WIDTH = NA_HEADS * NA_HEAD_DIM

ML_HEADS = 4
ML_QK = 128
ML_V = 256

N_EXPERTS = 32
TOP_K = 4
D_EXPERT = 512
SWIGLU_LIMIT = 7.0
SWIGLU_ALPHA = 1.702

LANE = 128

EV_Z = 0
EV_XBC = 1024
EV_MLA = 3072
EV_DT = 3968
EV_PAD_COLS = 4096
OD_GATE = 6144
OD_PAD_COLS = 6272

VMEM_LIMIT = 56 * 1024 * 1024


def _cparams(sem):
    return pltpu.CompilerParams(dimension_semantics=sem, vmem_limit_bytes=VMEM_LIMIT)


def _rms(x, axis=-1):
    return x * lax.rsqrt(jnp.mean(x * x, axis=axis, keepdims=True) + EPS)


def _softplus(x):
    return jnp.maximum(x, 0.0) + jnp.log(1.0 + jnp.exp(-jnp.abs(x)))


def _sigmoid(x):
    return 1.0 / (1.0 + jnp.exp(-x))


def _dot(a, b):
    return jnp.dot(a, b, preferred_element_type=F32)


def _dot_nt(a, b):
    return lax.dot_general(a, b, (((1,), (1,)), ((), ())), preferred_element_type=F32)


def _mods_kernel(a_ref, w_ref, b_ref, o_ref):
    a = a_ref[...]
    a = a * _sigmoid(a)
    o_ref[0] = _dot(a.astype(BF16), w_ref[0].astype(BF16)) + b_ref[0]


def _mods(cond, mod_w, mod_b):
    depth, d, n = mod_w.shape
    tn = 1024
    return pl.pallas_call(
        _mods_kernel,
        grid=(depth, n // tn),
        in_specs=[
            pl.BlockSpec((8, d), lambda l, j: (0, 0)),
            pl.BlockSpec((1, d, tn), lambda l, j: (l, 0, j)),
            pl.BlockSpec((1, 1, tn), lambda l, j: (l, 0, j)),
        ],
        out_specs=pl.BlockSpec((1, 8, tn), lambda l, j: (l, 0, j)),
        out_shape=jax.ShapeDtypeStruct((depth, 8, n), F32),
        compiler_params=_cparams(("parallel", "parallel")),
        name="mods",
    )(cond, mod_w, mod_b.reshape(depth, 1, n))


def _pick_mod(i, tiles_per_batch, tm, mc, ml):
    row = lax.broadcasted_iota(jnp.int32, (tm, 1), 0)
    is_ctx = jnp.logical_and(i % tiles_per_batch == 0, row < CTX_LEN)
    return jnp.where(is_ctx, mc, ml)


def _modulated(x, g, i, tiles_per_batch, mc_ref, ml_ref, sh_idx, sc_idx):
    tm = x.shape[0]
    y = _rms(x) * g
    shift = _pick_mod(i, tiles_per_batch, tm, mc_ref[sh_idx:sh_idx + 1, :], ml_ref[0, sh_idx:sh_idx + 1, :])
    scale = _pick_mod(i, tiles_per_batch, tm, mc_ref[sc_idx:sc_idx + 1, :], ml_ref[0, sc_idx:sc_idx + 1, :])
    return y * (1.0 + scale) + shift


def _modmm_kernel(x_ref, g_ref, mc_ref, ml_ref, w_ref, o_ref, h_scr, *, tiles_per_batch, sh_idx, sc_idx):
    i = pl.program_id(0)

    @pl.when(pl.program_id(1) == 0)
    def _():
        h = _modulated(x_ref[...], g_ref[...], i, tiles_per_batch, mc_ref, ml_ref, sh_idx, sc_idx)
        h_scr[...] = h.astype(BF16)

    o_ref[...] = _dot(h_scr[...], w_ref[...])


def _modmm(xs, g, mod_c, mod_l, w, *, nt, sh_idx, sc_idx, tn):
    t, d = xs.shape
    n = w.shape[1]
    tm = 768
    tpb = nt // tm
    return pl.pallas_call(
        functools.partial(_modmm_kernel, tiles_per_batch=tpb, sh_idx=sh_idx, sc_idx=sc_idx),
        grid=(t // tm, n // tn),
        in_specs=[
            pl.BlockSpec((tm, d), lambda i, j: (i, 0)),
            pl.BlockSpec((1, d), lambda i, j: (0, 0)),
            pl.BlockSpec((6, d), lambda i, j: (0, 0)),
            pl.BlockSpec((1, 6, d), lambda i, j: (i // tpb, 0, 0)),
            pl.BlockSpec((d, tn), lambda i, j: (0, j)),
        ],
        out_specs=pl.BlockSpec((tm, tn), lambda i, j: (i, j)),
        out_shape=jax.ShapeDtypeStruct((t, n), F32),
        scratch_shapes=[pltpu.VMEM((tm, d), BF16)],
        compiler_params=_cparams(("parallel", "arbitrary")),
        name="modmm",
    )(xs, g.reshape(1, d), mod_c, mod_l, w)


def _outproj_kernel(a1_ref, a2_ref, w1_ref, w2_ref, x_ref, mc_ref, ml_ref, o_ref, *, tiles_per_batch, g_idx):
    i = pl.program_id(0)
    acc = _dot(a1_ref[...], w1_ref[...]) + _dot(a2_ref[...], w2_ref[...])
    gate = _pick_mod(i, tiles_per_batch, acc.shape[0], mc_ref[g_idx:g_idx + 1, :], ml_ref[0, g_idx:g_idx + 1, :])
    o_ref[...] = x_ref[...] + gate * acc


def _outproj(a1, a2, w, xs, mod_c, mod_l, *, nt, g_idx):
    t, d = xs.shape
    k1 = a1.shape[1]
    k2 = a2.shape[1]
    tm, tn = 768, 1024
    tpb = nt // tm
    assert k1 == k2
    return pl.pallas_call(
        functools.partial(_outproj_kernel, tiles_per_batch=tpb, g_idx=g_idx),
        grid=(t // tm, d // tn),
        in_specs=[
            pl.BlockSpec((tm, k1), lambda i, j: (i, 0)),
            pl.BlockSpec((tm, k2), lambda i, j: (i, 0)),
            pl.BlockSpec((k1, tn), lambda i, j: (0, j)),
            pl.BlockSpec((k2, tn), lambda i, j: (1, j)),
            pl.BlockSpec((tm, tn), lambda i, j: (i, j)),
            pl.BlockSpec((6, tn), lambda i, j: (0, j)),
            pl.BlockSpec((1, 6, tn), lambda i, j: (i // tpb, 0, j)),
        ],
        out_specs=pl.BlockSpec((tm, tn), lambda i, j: (i, j)),
        out_shape=jax.ShapeDtypeStruct((t, d), F32),
        compiler_params=_cparams(("parallel", "parallel")),
        name="outproj",
    )(a1, a2, w, w, xs, mod_c, mod_l)


def _mla_prep_kernel(u_ref, lnq_ref, lnkv_ref, wqn_ref, wqr_ref, wqs_ref, wkk_ref, wkv_ref,
                     gq_ref, gk_ref, cos_ref, sin_ref, q_ref, k_ref, v_ref):
    u = u_ref[...]
    cq = u[:, :MLA_Q_RANK]
    ckv = u[:, MLA_Q_RANK:MLA_Q_RANK + MLA_KV_RANK]
    o = MLA_Q_RANK + MLA_KV_RANK
    kpe = u[:, o:o + MLA_ROPE]
    kpe_s = u[:, o + MLA_ROPE:o + 2 * MLA_ROPE]
    cqn = (_rms(cq) * lnq_ref[...]).astype(BF16)
    ckvn = (_rms(ckv) * lnkv_ref[...]).astype(BF16)
    qn = _dot(cqn, wqn_ref[...])
    qr = _dot(cqn, wqr_ref[...])
    qs = _dot(cqn, wqs_ref[...])
    kn = _dot(ckvn, wkk_ref[...])
    vv = _dot(ckvn, wkv_ref[...])
    cos = cos_ref[...]
    sin = sin_ref[...]
    gq = gq_ref[...]
    gk = gk_ref[...]
    gq_n, gq_r, gq_s = gq[:, :128], gq[:, 128:192], gq[:, 192:256]
    gk_n, gk_r, gk_s = gk[:, :128], gk[:, 128:192], gk[:, 192:256]
    kpe_rot = (kpe * gk_r) * cos + (kpe_s * gk_s) * sin
    kpe_sq = jnp.sum(kpe * kpe, axis=-1, keepdims=True)
    scale = MLA_QK ** -0.5
    for h in range(MLA_HEADS):
        qn_h = qn[:, h * 128:(h + 1) * 128]
        qr_h = qr[:, h * 64:(h + 1) * 64]
        qs_h = qs[:, h * 64:(h + 1) * 64]
        ms = (jnp.sum(qn_h * qn_h, axis=-1, keepdims=True) + jnp.sum(qr_h * qr_h, axis=-1, keepdims=True)) / MLA_QK
        r = lax.rsqrt(ms + EPS) * scale
        q_ref[0, h, :, 0:128] = (qn_h * gq_n * r).astype(BF16)
        q_ref[0, h, :, 128:192] = (((qr_h * gq_r) * cos + (qs_h * gq_s) * sin) * r).astype(BF16)
        kn_h = kn[:, h * 128:(h + 1) * 128]
        ms = (jnp.sum(kn_h * kn_h, axis=-1, keepdims=True) + kpe_sq) / MLA_QK
        r = lax.rsqrt(ms + EPS)
        k_ref[0, h, :, 0:128] = (kn_h * gk_n * r).astype(BF16)
        k_ref[0, h, :, 128:192] = (kpe_rot * r).astype(BF16)
        v_ref[0, h] = vv[:, h * 128:(h + 1) * 128].astype(BF16)


def _mla_prep(u, lnq, lnkv, wqn, wqr, wqs, wkk, wkv, gq, gk, cos, sin, *, batch, nt):
    tq = 256
    npb = nt // tq
    full = lambda a: pl.BlockSpec(a.shape, lambda b, i: (0,) * a.ndim)
    hm = lambda dd: pl.BlockSpec((1, MLA_HEADS, tq, dd), lambda b, i: (b, 0, i, 0))
    return pl.pallas_call(
        _mla_prep_kernel,
        grid=(batch, npb),
        in_specs=[
            pl.BlockSpec((tq, 1024), lambda b, i: (b * npb + i, EV_MLA // 1024)),
            full(lnq), full(lnkv), full(wqn), full(wqr), full(wqs), full(wkk), full(wkv), full(gq), full(gk),
            pl.BlockSpec((tq, MLA_ROPE), lambda b, i: (i, 0)),
            pl.BlockSpec((tq, MLA_ROPE), lambda b, i: (i, 0)),
        ],
        out_specs=[hm(MLA_QK), hm(MLA_QK), hm(MLA_V)],
        out_shape=[
            jax.ShapeDtypeStruct((batch, MLA_HEADS, nt, MLA_QK), BF16),
            jax.ShapeDtypeStruct((batch, MLA_HEADS, nt, MLA_QK), BF16),
            jax.ShapeDtypeStruct((batch, MLA_HEADS, nt, MLA_V), BF16),
        ],
        compiler_params=_cparams(("parallel", "parallel")),
        name="mla_prep",
    )(u, lnq, lnkv, wqn, wqr, wqs, wkk, wkv, gq, gk, cos, sin)


def _softmax_pv(s, v):
    m = jnp.max(s, axis=-1, keepdims=True)
    p = jnp.exp(s - m)
    l = jnp.sum(p, axis=-1, keepdims=True)
    return _dot(p.astype(BF16), v) / l


def _mla_attn_kernel(q_ref, k_ref, v_ref, o_ref, *, nt):
    qi = pl.program_id(2)
    q = q_ref[0, 0]

    def attend(nk):
        s = _dot_nt(q, k_ref[0, 0, :nk, :])
        o_ref[...] = _softmax_pv(s, v_ref[0, 0, :nk, :]).astype(BF16)

    @pl.when(qi == 0)
    def _():
        attend(CTX_LEN)

    @pl.when(qi > 0)
    def _():
        attend(nt)


def _mla_attn(q, k, v, *, batch, nt):
    tq = CTX_LEN
    npb = nt // tq
    return pl.pallas_call(
        functools.partial(_mla_attn_kernel, nt=nt),
        grid=(batch, MLA_HEADS, npb),
        in_specs=[
            pl.BlockSpec((1, 1, tq, MLA_QK), lambda b, h, i: (b, h, i, 0)),
            pl.BlockSpec((1, 1, nt, MLA_QK), lambda b, h, i: (b, h, 0, 0)),
            pl.BlockSpec((1, 1, nt, MLA_V), lambda b, h, i: (b, h, 0, 0)),
        ],
        out_specs=pl.BlockSpec((tq, MLA_V), lambda b, h, i: (b * npb + i, h)),
        out_shape=jax.ShapeDtypeStruct((batch * nt, MLA_HEADS * MLA_V), BF16),
        compiler_params=_cparams(("parallel", "parallel", "parallel")),
        name="mla_attn",
    )(q, k, v)


def _conv_kernel(x_ref, w_ref, b_ref, o_ref, *, nt):
    x = x_ref[...]
    row = lax.broadcasted_iota(jnp.int32, (nt, 1), 0)
    is_ctx = row < CTX_LEN
    p = jnp.where(is_ctx, row, row - CTX_LEN)
    n = jnp.where(is_ctx, CTX_LEN, nt - CTX_LEN)
    acc = b_ref[...] + x * w_ref[SSM_CONV // 2:SSM_CONV // 2 + 1, :]
    for j in range(SSM_CONV):
        off = j - SSM_CONV // 2
        if off == 0:
            continue
        shifted = pltpu.roll(x, (-off) % nt, 0)
        ok = jnp.logical_and(p + off >= 0, p + off < n)
        acc = acc + jnp.where(ok, shifted, 0.0) * w_ref[j:j + 1, :]
    o_ref[...] = acc * _sigmoid(acc)


def _conv(u, w, b, *, batch, nt):
    ct = 256
    c = w.shape[1]
    return pl.pallas_call(
        functools.partial(_conv_kernel, nt=nt),
        grid=(batch, c // ct),
        in_specs=[
            pl.BlockSpec((nt, ct), lambda bb, j: (bb, EV_XBC // ct + j)),
            pl.BlockSpec((SSM_CONV, ct), lambda bb, j: (0, j)),
            pl.BlockSpec((1, ct), lambda bb, j: (0, j)),
        ],
        out_specs=pl.BlockSpec((nt, ct), lambda bb, j: (bb, j)),
        out_shape=jax.ShapeDtypeStruct((batch * nt, c), F32),
        compiler_params=_cparams(("parallel", "parallel")),
        name="ssm_conv",
    )(u, w, b.reshape(1, c))


def _tri(reverse):
    r = lax.broadcasted_iota(jnp.int32, (CHUNK, CHUNK), 0)
    c = lax.broadcasted_iota(jnp.int32, (CHUNK, CHUNK), 1)
    return (c >= r) if reverse else (c <= r)


def _cumsums(a, mask):
    tri = mask.astype(F32)
    cs = jnp.dot(tri, a, precision=HIGHEST, preferred_element_type=F32)
    cs_t = lax.dot_general(a.T, tri, (((1,), (1,)), ((), ())), precision=HIGHEST, preferred_element_type=F32)
    return cs, cs_t


def _select_cols(v, e, terms):
    acc = None
    r = v
    for _ in range(terms):
        p = r.astype(BF16)
        r = r - p.astype(F32)
        acc = _dot(p, e) if acc is None else acc + _dot(p, e)
    return acc


def _ssd_dir(xa, dtr, bias, a_neg, e_head, e_blk, st_ref, y_ref, d, reverse):
    mask = _tri(reverse)
    xs = xa[:, :SSM_INNER]
    bm = xa[:, SSM_INNER:SSM_INNER + SSM_GROUPS * SSM_STATE]
    cm = xa[:, SSM_INNER + SSM_GROUPS * SSM_STATE:]
    dt = _softplus(dtr + bias)
    a = dt * a_neg
    cs, cs_t = _cumsums(a, mask)
    end = 0 if reverse else CHUNK - 1
    tot = jnp.broadcast_to(cs[end:end + 1, :], (8, LANE))
    xg = xs * _select_cols(dt, e_head, 2)
    y_coef = _select_cols(jnp.exp(cs), e_head, 2)
    xgd = (xg * _select_cols(jnp.exp(tot[0:1] - cs), e_head, 2)).astype(BF16)
    s_coef = _select_cols(jnp.exp(tot), e_head, 2)[0:1, :]
    cs_b = _select_cols(cs, e_blk, 3)
    xg = xg.astype(BF16)
    first_half = lax.broadcasted_iota(jnp.int32, (CHUNK, LANE), 1) < SSM_HEAD_DIM
    rep = SSM_HEADS // SSM_GROUPS
    gw = rep * SSM_HEAD_DIM
    for g in range(SSM_GROUPS):
        bg = bm[:, g * SSM_STATE:(g + 1) * SSM_STATE]
        cg = cm[:, g * SSM_STATE:(g + 1) * SSM_STATE].astype(BF16)
        bg_t = bg.T.astype(BF16)
        cb = _dot(cg, bg_t)
        st = st_ref[g]
        y_o = _dot(cg, st.astype(BF16)) * y_coef[:, g * gw:(g + 1) * gw]
        for pr in range(rep // 2):
            h0 = g * rep + 2 * pr
            x_pair = xg[:, h0 * SSM_HEAD_DIM:(h0 + 2) * SSM_HEAD_DIM]
            ys = []
            for h in (h0, h0 + 1):
                col = d * SSM_HEADS + h
                lm = jnp.exp(jnp.where(mask, cs_b[:, h * LANE:(h + 1) * LANE] - cs_t[col:col + 1, :], NEG))
                ys.append(_dot((cb * lm).astype(BF16), x_pair))
            y_d = jnp.where(first_half, ys[0], ys[1])
            y_ref[:, h0 * SSM_HEAD_DIM:(h0 + 2) * SSM_HEAD_DIM] = (
                y_d + y_o[:, 2 * pr * SSM_HEAD_DIM:(2 * pr + 2) * SSM_HEAD_DIM])
        st_ref[g] = s_coef[:, g * gw:(g + 1) * gw] * st + _dot(bg_t, xgd[:, g * gw:(g + 1) * gw])


def _ssd_kernel(xf_ref, xb_ref, dtf_ref, dtb_ref, bias_ref, alog_ref, eh_ref, eb_ref, yf_ref, yb_ref, stf_ref, stb_ref):
    @pl.when(pl.program_id(1) == 0)
    def _():
        stf_ref[...] = jnp.zeros_like(stf_ref)
        stb_ref[...] = jnp.zeros_like(stb_ref)

    bias = bias_ref[...]
    a_neg = -jnp.exp(alog_ref[...])
    _ssd_dir(xf_ref[...], dtf_ref[...], bias, a_neg, eh_ref[0], eb_ref[0], stf_ref, yf_ref, 0, False)
    _ssd_dir(xb_ref[...], dtb_ref[...], bias, a_neg, eh_ref[1], eb_ref[1], stb_ref, yb_ref, 1, True)


def _bwd_chunk(s, nc):
    ncc = CTX_LEN // CHUNK
    return jnp.where(s < ncc, ncc - 1 - s, nc + ncc - 1 - s)


def _ssd_select_tables():
    col = np.arange(LANE)[None, :, None]
    d = np.arange(2)[:, None, None]
    head_of_lane = (np.arange(SSM_INNER) // SSM_HEAD_DIM)[None, None, :]
    head_of_blk = (np.arange(SSM_HEADS * LANE) // LANE)[None, None, :]
    e_head = (col == d * SSM_HEADS + head_of_lane).astype(np.float32)
    e_blk = (col == d * SSM_HEADS + head_of_blk).astype(np.float32)
    return jnp.asarray(e_head, dtype=BF16), jnp.asarray(e_blk, dtype=BF16)


def _ssd(xa, u, dt_bias, a_log, *, batch, nt):
    nc = nt // CHUNK
    c = xa.shape[1]
    dt_blk = EV_DT // LANE
    fwd = lambda b, s: (b * nc + s, 0)
    bwd = lambda b, s: (b * nc + _bwd_chunk(s, nc), 0)
    e_head, e_blk = _ssd_select_tables()
    gw = SSM_INNER // SSM_GROUPS
    return pl.pallas_call(
        _ssd_kernel,
        grid=(batch, nc),
        in_specs=[
            pl.BlockSpec((CHUNK, c), fwd),
            pl.BlockSpec((CHUNK, c), bwd),
            pl.BlockSpec((CHUNK, LANE), lambda b, s: (b * nc + s, dt_blk)),
            pl.BlockSpec((CHUNK, LANE), lambda b, s: (b * nc + _bwd_chunk(s, nc), dt_blk)),
            pl.BlockSpec((1, LANE), lambda b, s: (0, 0)),
            pl.BlockSpec((1, LANE), lambda b, s: (0, 0)),
            pl.BlockSpec(e_head.shape, lambda b, s: (0, 0, 0)),
            pl.BlockSpec(e_blk.shape, lambda b, s: (0, 0, 0)),
        ],
        out_specs=[pl.BlockSpec((CHUNK, SSM_INNER), fwd), pl.BlockSpec((CHUNK, SSM_INNER), bwd)],
        out_shape=[jax.ShapeDtypeStruct((batch * nt, SSM_INNER), F32)] * 2,
        scratch_shapes=[pltpu.VMEM((SSM_GROUPS, SSM_STATE, gw), F32)] * 2,
        compiler_params=_cparams(("parallel", "arbitrary")),
        name="ssd_scan",
    )(xa, xa, u, u, dt_bias, a_log, e_head, e_blk)


def _ssm_out_kernel(yf_ref, yb_ref, xs_ref, z_ref, dsk_ref, g_ref, o_ref):
    z = z_ref[...]
    y = yf_ref[...] + yb_ref[...] + dsk_ref[...] * xs_ref[...]
    y = y * (z * _sigmoid(z))
    o_ref[...] = (_rms(y) * g_ref[...]).astype(BF16)


def _ssm_out(yf, yb, xa, u, d_skip, g):
    t = yf.shape[0]
    tm = 256
    blk = pl.BlockSpec((tm, SSM_INNER), lambda i: (i, 0))
    vec = pl.BlockSpec((1, SSM_INNER), lambda i: (0, 0))
    return pl.pallas_call(
        _ssm_out_kernel,
        grid=(t // tm,),
        in_specs=[blk, blk, blk, blk, vec, vec],
        out_specs=blk,
        out_shape=jax.ShapeDtypeStruct((t, SSM_INNER), BF16),
        compiler_params=_cparams(("parallel",)),
        name="ssm_out",
    )(yf, yb, xa, u, d_skip, g)


def _na_kernel(q_ref, k_ref, v_ref, gq_ref, gk_ref, bias_ref, o_ref, qs, ks, vs, *, nt):
    scale = NA_HEAD_DIM ** -0.5
    qs[...] = (_rms(q_ref[...]) * (gq_ref[...] * scale)).astype(BF16)
    ks[...] = (_rms(k_ref[...]) * gk_ref[...]).astype(BF16)
    vs[...] = v_ref[...].astype(BF16)
    kc = ks[0:CTX_LEN, :]
    vc = vs[0:CTX_LEN, :]
    o_ref[0:CTX_LEN, :] = _softmax_pv(_dot_nt(qs[0:CTX_LEN, :], kc), vc).astype(BF16)
    plan = _na_plan((nt - CTX_LEN) // GRID_W)
    gq_rows = NA_GROUP * GRID_W
    for g, (u0, var) in enumerate(zip(plan["u0"], plan["variant"])):
        q0 = CTX_LEN + g * gq_rows
        k0 = CTX_LEN + u0 * GRID_W
        q = qs[q0:q0 + gq_rows, :]
        kb = ks[k0:k0 + NA_BAND * GRID_W, :]
        vb = vs[k0:k0 + NA_BAND * GRID_W, :]
        s_loc = _dot_nt(q, kb) + bias_ref[0, var]
        s_ctx = _dot_nt(q, kc)
        m = jnp.maximum(jnp.max(s_loc, axis=-1, keepdims=True), jnp.max(s_ctx, axis=-1, keepdims=True))
        p_loc = jnp.exp(s_loc - m)
        p_ctx = jnp.exp(s_ctx - m)
        l = jnp.sum(p_loc, axis=-1, keepdims=True) + jnp.sum(p_ctx, axis=-1, keepdims=True)
        o = (_dot(p_loc.astype(BF16), vb) + _dot(p_ctx.astype(BF16), vc)) / l
        o_ref[q0:q0 + gq_rows, :] = o.astype(BF16)


NA_GROUP = 4
NA_BAND = NA_ROWS + NA_GROUP - 1


@functools.lru_cache(maxsize=None)
def _na_plan(rows):
    ngroups = rows // NA_GROUP
    u0s, pats = [], []
    for g in range(ngroups):
        r = g * NA_GROUP + np.arange(NA_GROUP)
        r0 = np.clip(r - NA_ROWS // 2, 0, rows - NA_ROWS)
        u0 = int(np.clip(r0.min(), 0, rows - NA_BAND))
        key_row = u0 + np.arange(NA_BAND)
        valid = (key_row[None, :] >= r0[:, None]) & (key_row[None, :] < r0[:, None] + NA_ROWS)
        assert valid.sum(axis=1).min() == NA_ROWS
        dr0 = u0 - r + NA_ROWS - 1
        u0s.append(u0)
        pats.append((tuple(dr0.tolist()), valid.tobytes()))
    uniq = sorted(set(pats), key=pats.index)
    variant = [uniq.index(p) for p in pats]
    dr0 = np.array([u[0] for u in uniq])
    valid = np.stack([np.frombuffer(u[1], dtype=bool).reshape(NA_GROUP, NA_BAND) for u in uniq])
    return {"u0": u0s, "variant": variant, "dr0": dr0, "valid": valid}


def _na(u, gq, gk, bias, *, batch, nt):
    hd = NA_HEAD_DIM
    return pl.pallas_call(
        functools.partial(_na_kernel, nt=nt),
        grid=(batch, NA_HEADS),
        in_specs=[
            pl.BlockSpec((nt, hd), lambda b, h: (b, h)),
            pl.BlockSpec((nt, hd), lambda b, h: (b, NA_HEADS + h)),
            pl.BlockSpec((nt, hd), lambda b, h: (b, 2 * NA_HEADS + h)),
            pl.BlockSpec((1, hd), lambda b, h: (0, 0)),
            pl.BlockSpec((1, hd), lambda b, h: (0, 0)),
            pl.BlockSpec((1,) + bias.shape[1:], lambda b, h: (h, 0, 0, 0)),
        ],
        out_specs=pl.BlockSpec((nt, hd), lambda b, h: (b, h)),
        out_shape=jax.ShapeDtypeStruct((batch * nt, NA_WIDTH), BF16),
        scratch_shapes=[pltpu.VMEM((nt, hd), BF16)] * 3,
        compiler_params=_cparams(("parallel", "parallel")),
        name="na_attn",
    )(u, u, u, gq, gk, bias)


def _na_bias_table(rpb, rows):
    plan = _na_plan(rows)
    col = np.arange(GRID_W)
    col_start = np.clip(col - NA_COLS // 2, 0, GRID_W - NA_COLS)
    col_ok = (col[None, :] >= col_start[:, None]) & (col[None, :] < col_start[:, None] + NA_COLS)
    dc = np.clip(col[None, :] - col[:, None], -(NA_COLS - 1), NA_COLS - 1) + NA_COLS - 1
    onehot_dc = (dc[:, :, None] == np.arange(2 * NA_COLS - 1)).astype(np.float32)
    m = jnp.einsum("hab,cdb->hacd", rpb.astype(F32), jnp.asarray(onehot_dc), precision=HIGHEST)
    m = jnp.pad(m, ((0, 0), (NA_BAND, NA_BAND), (0, 0), (0, 0)))
    out = []
    for v in range(plan["dr0"].shape[0]):
        per_j = [m[:, NA_BAND + int(a):NA_BAND + int(a) + NA_BAND] for a in plan["dr0"][v]]
        b = jnp.stack(per_j, axis=1).transpose(0, 1, 3, 2, 4)
        ok = plan["valid"][v][:, None, :, None] & col_ok[None, :, None, :]
        b = jnp.where(jnp.asarray(ok)[None], b, NEG)
        out.append(b.reshape(NA_HEADS, NA_GROUP * GRID_W, NA_BAND * GRID_W))
    return jnp.stack(out, axis=1).astype(F32)


def _mlstm_dir(q, k, v, gm, gb, ct_ref, n_ref, m_ref, h_ref, d, reverse):
    mask = _tri(reverse)
    g_all = gm + gb
    lf_all = jnp.minimum(g_all, 0.0) - jnp.log(1.0 + jnp.exp(-jnp.abs(g_all)))
    b_all, b_all_t = _cumsums(lf_all, mask)
    g_all_t = g_all.T
    end = 0 if reverse else CHUNK - 1
    for h in range(ML_HEADS):
        icol = d * ML_HEADS + h
        fcol = 2 * ML_HEADS + d * ML_HEADS + h
        b_col = b_all[:, fcol:fcol + 1]
        b_row = b_all_t[fcol:fcol + 1, :]
        li_col = g_all[:, icol:icol + 1]
        li_row = g_all_t[icol:icol + 1, :]
        m_prev = m_ref[h][0:1, 0:1]
        dmat = jnp.where(mask, b_col - b_row + li_row, NEG)
        inter = b_col + m_prev
        m_t = jnp.maximum(inter, jnp.max(dmat, axis=1, keepdims=True))
        qh = q[:, h * ML_QK:(h + 1) * ML_QK]
        kh = k[:, h * ML_QK:(h + 1) * ML_QK] * (ML_QK ** -0.5)
        vh = v[:, h * ML_V:(h + 1) * ML_V]
        qb = qh.astype(BF16)
        kb = kh.astype(BF16)
        s = _dot_nt(qb, kb) * jnp.exp(dmat - m_t)
        w_inter = jnp.exp(inter - m_t)
        ct = ct_ref[h]
        nvec = n_ref[h][0:1, :]
        num = _dot(s.astype(BF16), vh.astype(BF16)) + w_inter * _dot(qb, ct.astype(BF16))
        den = jnp.sum(s, axis=1, keepdims=True) + w_inter * jnp.sum(qh * nvec, axis=1, keepdims=True)
        h_ref[:, h * ML_V:(h + 1) * ML_V] = num / jnp.maximum(jnp.abs(den), jnp.exp(-m_t))
        b_end = b_col[end:end + 1, :]
        g_col = b_end - b_col + li_col
        m_new = jnp.maximum(b_end + m_prev, jnp.max(g_col, axis=0, keepdims=True))
        wk = jnp.exp(g_col - m_new)
        decay = jnp.exp(b_end + m_prev - m_new)
        ct_ref[h] = decay * ct + _dot(kh.T.astype(BF16), (wk * vh).astype(BF16))
        n_ref[h] = jnp.broadcast_to(decay * nvec + jnp.sum(wk * kh, axis=0, keepdims=True), (8, ML_QK))
        m_ref[h] = jnp.broadcast_to(m_new, (8, LANE))


def _mlstm_kernel(qf_ref, kf_ref, vf_ref, gf_ref, qb_ref, kb_ref, vb_ref, gbk_ref, gb_ref,
                  hf_ref, hb_ref, ctf, nf, mf, ctb, nb, mb):
    @pl.when(pl.program_id(1) == 0)
    def _():
        for r in (ctf, nf, mf, ctb, nb, mb):
            r[...] = jnp.zeros_like(r)

    gb = gb_ref[...]
    _mlstm_dir(qf_ref[...], kf_ref[...], vf_ref[...], gf_ref[...], gb, ctf, nf, mf, hf_ref, 0, False)
    _mlstm_dir(qb_ref[...], kb_ref[...], vb_ref[...], gbk_ref[...], gb, ctb, nb, mb, hb_ref, 1, True)


def _mlstm(u, gate_b, *, batch, nt):
    nc = nt // CHUNK
    wqk = ML_HEADS * ML_QK
    wv = ML_HEADS * ML_V
    qo, ko, vo, go = 3 * NA_WIDTH, 3 * NA_WIDTH + wqk, 3 * NA_WIDTH + 2 * wqk, OD_GATE

    def specs(chunk_of):
        return [
            pl.BlockSpec((CHUNK, wqk), lambda b, s: (b * nc + chunk_of(s), qo // wqk)),
            pl.BlockSpec((CHUNK, wqk), lambda b, s: (b * nc + chunk_of(s), ko // wqk)),
            pl.BlockSpec((CHUNK, wv), lambda b, s: (b * nc + chunk_of(s), vo // wv)),
            pl.BlockSpec((CHUNK, LANE), lambda b, s: (b * nc + chunk_of(s), go // LANE)),
        ]

    fwd_of = lambda s: s
    bwd_of = lambda s: _bwd_chunk(s, nc)
    state = [pltpu.VMEM((ML_HEADS, ML_QK, ML_V), F32), pltpu.VMEM((ML_HEADS, 8, ML_QK), F32),
             pltpu.VMEM((ML_HEADS, 8, LANE), F32)]
    return pl.pallas_call(
        _mlstm_kernel,
        grid=(batch, nc),
        in_specs=specs(fwd_of) + specs(bwd_of) + [pl.BlockSpec((1, LANE), lambda b, s: (0, 0))],
        out_specs=[pl.BlockSpec((CHUNK, wv), lambda b, s: (b * nc + s, 0)),
                   pl.BlockSpec((CHUNK, wv), lambda b, s: (b * nc + _bwd_chunk(s, nc), 0))],
        out_shape=[jax.ShapeDtypeStruct((batch * nt, wv), F32)] * 2,
        scratch_shapes=state + state,
        compiler_params=_cparams(("parallel", "arbitrary")),
        name="mlstm_scan",
    )(u, u, u, u, u, u, u, u, gate_b)


def _mlstm_out_kernel(hf_ref, hb_ref, o_ref_in, g_ref, o_ref):
    hs = hf_ref[...] + hb_ref[...]
    og = _sigmoid(o_ref_in[...])
    g = g_ref[...]
    for h in range(ML_HEADS):
        sl = slice(h * ML_V, (h + 1) * ML_V)
        o_ref[:, sl] = (_rms(hs[:, sl]) * g[:, sl] * og[:, sl]).astype(BF16)


def _mlstm_out(hf, hb, u, g):
    t = hf.shape[0]
    tm = 256
    wv = ML_HEADS * ML_V
    blk = pl.BlockSpec((tm, wv), lambda i: (i, 0))
    o_blk = (3 * NA_WIDTH + 2 * ML_HEADS * ML_QK + wv) // wv
    return pl.pallas_call(
        _mlstm_out_kernel,
        grid=(t // tm,),
        in_specs=[blk, blk, pl.BlockSpec((tm, wv), lambda i: (i, o_blk)), pl.BlockSpec((1, wv), lambda i: (0, 0))],
        out_specs=blk,
        out_shape=jax.ShapeDtypeStruct((t, wv), BF16),
        compiler_params=_cparams(("parallel",)),
        name="mlstm_out",
    )(hf, hb, u, g)


def _mod_row(is_ctx, mc_ref, ml_ref, idx):
    return jnp.where(is_ctx, mc_ref[idx:idx + 1, :], ml_ref[0, idx:idx + 1, :])


def _router_kernel(x_ref, g_ref, mc_ref, ml_ref, rw_ref, rb_ref, h_ref, ti_ref, tg_ref, *, skip_tiles, sh_idx, sc_idx):
    is_ctx = pl.program_id(1) + skip_tiles == 0
    y = _rms(x_ref[...]) * g_ref[...]
    h = y * (1.0 + _mod_row(is_ctx, mc_ref, ml_ref, sc_idx)) + _mod_row(is_ctx, mc_ref, ml_ref, sh_idx)
    h_ref[...] = h
    lg = jnp.dot(h, rw_ref[...], precision=HIGHEST, preferred_element_type=F32) + rb_ref[...]
    tm = lg.shape[0]
    lane = lax.broadcasted_iota(jnp.int32, (tm, LANE), 1)
    lg = jnp.where(lane < N_EXPERTS, lg, NEG)
    vals, idxs = [], []
    for _ in range(TOP_K):
        m = jnp.max(lg, axis=1, keepdims=True)
        idx = jnp.min(jnp.where(lg == m, lane, LANE), axis=1, keepdims=True)
        vals.append(m)
        idxs.append(idx)
        lg = jnp.where(lane == idx, 2.0 * NEG, lg)
    es = [jnp.exp(v - vals[0]) for v in vals]
    den = es[0] + es[1] + es[2] + es[3]
    ti = jnp.zeros((tm, LANE), jnp.int32)
    tg = jnp.zeros((tm, LANE), F32)
    for kk in range(TOP_K):
        ti = jnp.where(lane == kk, idxs[kk], ti)
        tg = jnp.where(lane == kk, es[kk] / den, tg)
    ti_ref[...] = ti
    tg_ref[...] = tg


def _router(xs, g, mod_c, mod_l, rw, rb, *, batch, nt, skip_tiles, sh_idx, sc_idx):
    t, d = xs.shape
    tm = CTX_LEN
    tpb = nt // tm
    used = tpb - skip_tiles
    tc = batch * used * tm
    src = lambda b, i: (b * tpb + skip_tiles + i, 0)
    dst = lambda b, i: (b * used + i, 0)
    return pl.pallas_call(
        functools.partial(_router_kernel, skip_tiles=skip_tiles, sh_idx=sh_idx, sc_idx=sc_idx),
        grid=(batch, used),
        in_specs=[
            pl.BlockSpec((tm, d), src),
            pl.BlockSpec((1, d), lambda b, i: (0, 0)),
            pl.BlockSpec((6, d), lambda b, i: (0, 0)),
            pl.BlockSpec((1, 6, d), lambda b, i: (b, 0, 0)),
            pl.BlockSpec((d, LANE), lambda b, i: (0, 0)),
            pl.BlockSpec((1, LANE), lambda b, i: (0, 0)),
        ],
        out_specs=[pl.BlockSpec((tm, d), dst), pl.BlockSpec((tm, LANE), dst), pl.BlockSpec((tm, LANE), dst)],
        out_shape=[jax.ShapeDtypeStruct((tc, d), F32),
                   jax.ShapeDtypeStruct((tc, LANE), jnp.int32),
                   jax.ShapeDtypeStruct((tc, LANE), F32)],
        compiler_params=_cparams(("parallel", "parallel")),
        name="router",
    )(xs, g.reshape(1, d), mod_c, mod_l, rw, rb)


EXPERT_TM = 256
DISPATCH_TM = 256
COMBINE_TM = 128
ROW_UNROLL = 4


def _dispatch_kernel(pos_ref, ztile_ref, h_ref, xs_hbm, zbuf, sem, zsem):
    i = pl.program_id(0)

    @pl.when(i == 0)
    def _():
        zbuf[...] = jnp.zeros_like(zbuf)

        def zcopy(e):
            return pltpu.make_async_copy(zbuf, xs_hbm.at[pl.ds(ztile_ref[e] * EXPERT_TM, EXPERT_TM)], zsem)

        def start(e, c):
            @pl.when(ztile_ref[e] >= 0)
            def _():
                zcopy(e).start()
            return c

        def wait(e, c):
            @pl.when(ztile_ref[e] >= 0)
            def _():
                zcopy(e).wait()
            return c

        lax.fori_loop(0, N_EXPERTS, start, 0)
        lax.fori_loop(0, N_EXPERTS, wait, 0)

    def row_copy(r, kk):
        p = pos_ref[0, 0, r * TOP_K + kk]
        return pltpu.make_async_copy(h_ref.at[pl.ds(r, 1)], xs_hbm.at[pl.ds(p, 1)], sem)

    def start(rb, c):
        for j in range(ROW_UNROLL):
            for kk in range(TOP_K):
                row_copy(rb * ROW_UNROLL + j, kk).start()
        return c

    lax.fori_loop(0, DISPATCH_TM // ROW_UNROLL, start, 0)
    for kk in range(TOP_K):
        pltpu.make_async_copy(h_ref, xs_hbm.at[pl.ds(0, DISPATCH_TM)], sem).wait()


def _dispatch(h, pos, ztile, *, n_rows):
    t, d = h.shape
    tm = DISPATCH_TM
    nsteps = t // tm
    return pl.pallas_call(
        _dispatch_kernel,
        grid=(nsteps,),
        in_specs=[
            pl.BlockSpec((1, 1, tm * TOP_K), lambda i: (i, 0, 0), memory_space=pltpu.SMEM),
            pl.BlockSpec(memory_space=pltpu.SMEM),
            pl.BlockSpec((tm, d), lambda i: (i, 0)),
        ],
        out_specs=pl.BlockSpec(memory_space=pl.ANY),
        out_shape=jax.ShapeDtypeStruct((n_rows, d), h.dtype),
        scratch_shapes=[pltpu.VMEM((EXPERT_TM, d), h.dtype), pltpu.SemaphoreType.DMA, pltpu.SemaphoreType.DMA],
        compiler_params=_cparams(("arbitrary",)),
        name="moe_dispatch",
    )(pos.reshape(nsteps, 1, tm * TOP_K), ztile, h)


def _expert_kernel(te_ref, nu_ref, x_ref, w1_ref, b1_ref, w2_ref, b2_ref, perm_ref, y_ref, w1b, w2b):
    i = pl.program_id(0)
    live = i < nu_ref[0]
    changed = jnp.logical_or(i == 0, te_ref[i] != te_ref[jnp.maximum(i - 1, 0)])

    @pl.when(jnp.logical_and(live, changed))
    def _():
        w1b[...] = w1_ref[0, 0].astype(BF16)
        w2b[...] = w2_ref[0, 0].astype(BF16)

    @pl.when(live)
    def _():
        u = _dot(x_ref[...].astype(BF16), w1b[...]) + b1_ref[0, 0]
        ud = _dot(u.astype(BF16), perm_ref[...])
        glu = jnp.minimum(ud[:, :D_EXPERT], SWIGLU_LIMIT)
        lin = jnp.clip(ud[:, D_EXPERT:], -SWIGLU_LIMIT, SWIGLU_LIMIT)
        act = glu * _sigmoid(SWIGLU_ALPHA * glu) * (lin + 1.0)
        y_ref[...] = _dot(act.astype(BF16), w2b[...]) + b2_ref[0, 0]


def _experts(xsorted, tile_expert, n_used, w1, b1, w2, b2, perm, *, layer):
    p, d = xsorted.shape
    tm = EXPERT_TM
    ntiles = p // tm
    de2 = 2 * D_EXPERT
    tile = lambda i, te, nu: (jnp.minimum(i, nu[0] - 1), 0)
    grid_spec = pltpu.PrefetchScalarGridSpec(
        num_scalar_prefetch=2,
        grid=(ntiles,),
        in_specs=[
            pl.BlockSpec((tm, d), tile),
            pl.BlockSpec((1, 1, d, de2), lambda i, te, nu: (layer, te[i], 0, 0)),
            pl.BlockSpec((1, 1, 1, de2), lambda i, te, nu: (layer, te[i], 0, 0)),
            pl.BlockSpec((1, 1, D_EXPERT, d), lambda i, te, nu: (layer, te[i], 0, 0)),
            pl.BlockSpec((1, 1, 1, d), lambda i, te, nu: (layer, te[i], 0, 0)),
            pl.BlockSpec((de2, de2), lambda i, te, nu: (0, 0)),
        ],
        out_specs=pl.BlockSpec((tm, d), tile),
        scratch_shapes=[pltpu.VMEM((d, de2), BF16), pltpu.VMEM((D_EXPERT, d), BF16)],
    )
    return pl.pallas_call(
        _expert_kernel,
        grid_spec=grid_spec,
        out_shape=jax.ShapeDtypeStruct((p, d), F32),
        compiler_params=_cparams(("arbitrary",)),
        name="moe_experts",
    )(tile_expert, n_used, xsorted, w1, b1.reshape(DEPTH, N_EXPERTS, 1, de2), w2, b2.reshape(DEPTH, N_EXPERTS, 1, d), perm)


def _combine_kernel(pos_ref, posn_ref, x_ref, tg_ref, mc_ref, ml_ref, y_hbm, o_ref, buf0, buf1, sem, *,
                    tiles_per_batch, ctx_tiles, g_idx):
    i = pl.program_id(0)
    n = pl.num_programs(0)
    tm = COMBINE_TM
    bufs = (buf0, buf1)

    def issue(pref, s):
        def body(rb, c):
            for j in range(ROW_UNROLL):
                r = rb * ROW_UNROLL + j
                for kk in range(TOP_K):
                    p = pref[0, 0, r * TOP_K + kk]
                    pltpu.make_async_copy(y_hbm.at[pl.ds(p, 1)], bufs[s].at[kk, pl.ds(r, 1)], sem.at[s]).start()
            return c
        lax.fori_loop(0, tm // ROW_UNROLL, body, 0)

    @pl.when(i == 0)
    def _():
        issue(pos_ref, 0)

    is_ctx = i % tiles_per_batch < ctx_tiles
    gate = _mod_row(is_ctx, mc_ref, ml_ref, g_idx)
    for s in range(2):
        @pl.when(i % 2 == s)
        def _():
            @pl.when(i + 1 < n)
            def _():
                issue(posn_ref, 1 - s)

            for kk in range(TOP_K):
                pltpu.make_async_copy(y_hbm.at[pl.ds(0, tm)], bufs[s].at[kk], sem.at[s]).wait()
            tg = tg_ref[...]
            acc = bufs[s][0] * tg[:, 0:1]
            for kk in range(1, TOP_K):
                acc = acc + bufs[s][kk] * tg[:, kk:kk + 1]
            o_ref[...] = x_ref[...] + gate * acc


def _combine(xs, tg, pos, ybuf, mod_c, mod_l, *, batch, nt, skip_tiles, g_idx):
    t, d = xs.shape
    tm = COMBINE_TM
    per = CTX_LEN // tm
    tpb_all = nt // tm
    tpb = tpb_all - skip_tiles * per
    nsteps = batch * tpb
    ctx_tiles = (1 - skip_tiles) * per
    pos3 = pos.reshape(nsteps, 1, tm * TOP_K)
    src = lambda i: ((i // tpb) * tpb_all + skip_tiles * per + i % tpb, 0)
    return pl.pallas_call(
        functools.partial(_combine_kernel, tiles_per_batch=tpb, ctx_tiles=ctx_tiles, g_idx=g_idx),
        grid=(nsteps,),
        in_specs=[
            pl.BlockSpec((1, 1, tm * TOP_K), lambda i: (i, 0, 0), memory_space=pltpu.SMEM),
            pl.BlockSpec((1, 1, tm * TOP_K), lambda i: (jnp.minimum(i + 1, nsteps - 1), 0, 0), memory_space=pltpu.SMEM),
            pl.BlockSpec((tm, d), src),
            pl.BlockSpec((tm, LANE), lambda i: (i, 0)),
            pl.BlockSpec((6, d), lambda i: (0, 0)),
            pl.BlockSpec((1, 6, d), lambda i: (i // tpb, 0, 0)),
            pl.BlockSpec(memory_space=pl.ANY),
        ],
        out_specs=pl.BlockSpec((tm, d), lambda i: (i, 0)),
        out_shape=jax.ShapeDtypeStruct((nsteps * tm, d), F32),
        scratch_shapes=[pltpu.VMEM((TOP_K, tm, d), F32)] * 2 + [pltpu.SemaphoreType.DMA((2,))],
        compiler_params=_cparams(("arbitrary",)),
        name="moe_combine",
    )(pos3, pos3, xs, tg, mod_c, mod_l, ybuf)


def _routing_tables(ti, n_tiles):
    top_i = ti[:, :TOP_K]
    t = top_i.shape[0]
    onehot = (top_i[:, :, None] == jnp.arange(N_EXPERTS, dtype=jnp.int32)).astype(jnp.int32)
    cnt = jnp.sum(onehot, axis=1)
    incl = jnp.cumsum(cnt, axis=0)
    size = incl[-1]
    tiles = (size + EXPERT_TM - 1) // EXPERT_TM
    tile_end = jnp.cumsum(tiles)
    tile_start = tile_end - tiles
    base = tile_start[None, :] * EXPERT_TM + (incl - cnt)
    pos = jnp.sum(onehot * base[:, None, :], axis=-1)
    n_used = tile_end[-1]
    tid = jnp.minimum(jnp.arange(n_tiles, dtype=jnp.int32), n_used - 1)
    tile_expert = jnp.sum((tile_end[None, :] <= tid[:, None]).astype(jnp.int32), axis=1)
    ztile = jnp.where(tiles > 0, tile_end - 1, -1).astype(jnp.int32)
    return pos.reshape(t * TOP_K).astype(jnp.int32), tile_expert, n_used.reshape(1).astype(jnp.int32), ztile


def _swiglu_perm():
    src = np.concatenate([np.arange(0, 2 * D_EXPERT, 2), np.arange(1, 2 * D_EXPERT, 2)])
    p = np.zeros((2 * D_EXPERT, 2 * D_EXPERT), np.float32)
    p[src, np.arange(2 * D_EXPERT)] = 1.0
    return jnp.asarray(p, dtype=BF16)


def _moe_layer(xs, l, norm_g, mod_c, mod_l, router_w, router_b, moe_w1, moe_b1, moe_w2, moe_b2, perm, *,
               batch, nt, skip_tiles):
    rw = jnp.pad(router_w[l], ((0, 0), (0, LANE - N_EXPERTS)))
    rb = jnp.pad(router_b[l], (0, LANE - N_EXPERTS)).reshape(1, LANE)
    h, ti, tg = _router(xs, norm_g, mod_c, mod_l, rw, rb, batch=batch, nt=nt, skip_tiles=skip_tiles, sh_idx=3, sc_idx=4)
    n_tiles = h.shape[0] * TOP_K // EXPERT_TM + N_EXPERTS
    pos, tile_expert, n_used, ztile = _routing_tables(ti, n_tiles)
    xsorted = _dispatch(h, pos, ztile, n_rows=n_tiles * EXPERT_TM)
    ybuf = _experts(xsorted, tile_expert, n_used, moe_w1, moe_b1, moe_w2, moe_b2, perm, layer=l)
    return _combine(xs, tg, pos, ybuf, mod_c, mod_l, batch=batch, nt=nt, skip_tiles=skip_tiles, g_idx=5)


_ROPE_SWAP = np.concatenate([np.arange(16, 32), np.arange(0, 16), np.arange(48, 64), np.arange(32, 48)])


def _rope_tables(n_lat):
    pos = jnp.arange(n_lat)
    row = (pos // GRID_W).astype(F32)
    col = (pos % GRID_W).astype(F32)
    axis_dim = MLA_ROPE // 2
    inv_freq = ROPE_BASE ** (-jnp.arange(0, axis_dim, 2, dtype=F32) / axis_dim)
    ar, ac = row[:, None] * inv_freq, col[:, None] * inv_freq
    cos = jnp.concatenate([jnp.cos(ar), jnp.cos(ar), jnp.cos(ac), jnp.cos(ac)], axis=-1)
    sin = jnp.concatenate([-jnp.sin(ar), jnp.sin(ar), -jnp.sin(ac), jnp.sin(ac)], axis=-1)
    cos = jnp.concatenate([jnp.ones((CTX_LEN, MLA_ROPE), F32), cos], axis=0)
    sin = jnp.concatenate([jnp.zeros((CTX_LEN, MLA_ROPE), F32), sin], axis=0)
    return cos, sin


def _ev_w_in_layout(w):
    d = w.shape[0]
    o = 0
    mla = w[:, o:o + MLA_Q_RANK + MLA_KV_RANK + MLA_ROPE]; o += MLA_Q_RANK + MLA_KV_RANK + MLA_ROPE
    z = w[:, o:o + SSM_INNER]; o += SSM_INNER
    xbc = w[:, o:o + SSM_XBC]; o += SSM_XBC
    dt = w[:, o:o + 2 * SSM_HEADS]
    kpe = mla[:, MLA_Q_RANK + MLA_KV_RANK:]
    pad = jnp.zeros((d, EV_PAD_COLS - EV_DT - 2 * SSM_HEADS), w.dtype)
    return jnp.concatenate([z, xbc, mla, kpe[:, _ROPE_SWAP], dt, pad], axis=1).astype(BF16)


def _pad_lane(v):
    return jnp.pad(v.reshape(-1), (0, LANE - v.size)).reshape(1, LANE).astype(F32)


def _even_mixer(xs, l, i, norm_g, mod_c, mod_l, ev_w_in, ev_w_out, mla_ln_q, mla_ln_kv, mla_w_uq, mla_w_ukv,
                mla_q_norm, mla_k_norm, ssm_conv_w, ssm_conv_b, ssm_a_log, ssm_dt_bias, ssm_d, ssm_norm,
                cos, sin, *, batch, nt):
    u = _modmm(xs, norm_g, mod_c, mod_l, _ev_w_in_layout(ev_w_in[i]), nt=nt, sh_idx=0, sc_idx=1, tn=1024)
    wq = mla_w_uq[i].reshape(MLA_Q_RANK, MLA_HEADS, MLA_QK)
    wqn = wq[:, :, :MLA_NOPE].reshape(MLA_Q_RANK, -1).astype(BF16)
    wqr = wq[:, :, MLA_NOPE:]
    wqs = wqr[:, :, _ROPE_SWAP].reshape(MLA_Q_RANK, -1).astype(BF16)
    wqr = wqr.reshape(MLA_Q_RANK, -1).astype(BF16)
    wkv = mla_w_ukv[i].reshape(MLA_KV_RANK, MLA_HEADS, MLA_NOPE + MLA_V)
    wkk = wkv[:, :, :MLA_NOPE].reshape(MLA_KV_RANK, -1).astype(BF16)
    wvv = wkv[:, :, MLA_NOPE:].reshape(MLA_KV_RANK, -1).astype(BF16)
    gq = mla_q_norm[i]
    gk = mla_k_norm[i]
    gq3 = jnp.concatenate([gq, gq[MLA_NOPE:][_ROPE_SWAP]]).reshape(1, -1)
    gk3 = jnp.concatenate([gk, gk[MLA_NOPE:][_ROPE_SWAP]]).reshape(1, -1)
    q, k, v = _mla_prep(u, mla_ln_q[i].reshape(1, -1), mla_ln_kv[i].reshape(1, -1), wqn, wqr, wqs, wkk, wvv,
                        gq3, gk3, cos, sin, batch=batch, nt=nt)
    o_mla = _mla_attn(q, k, v, batch=batch, nt=nt)
    xa = _conv(u, ssm_conv_w[i], ssm_conv_b[i], batch=batch, nt=nt)
    yf, yb = _ssd(xa, u, _pad_lane(ssm_dt_bias[i]), _pad_lane(ssm_a_log[i]), batch=batch, nt=nt)
    d_skip = jnp.repeat(ssm_d[i], SSM_HEAD_DIM).reshape(1, SSM_INNER)
    o_ssm = _ssm_out(yf, yb, xa, u, d_skip, ssm_norm[i].reshape(1, SSM_INNER))
    return _outproj(o_mla, o_ssm, ev_w_out[i].astype(BF16), xs, mod_c, mod_l, nt=nt, g_idx=2)


def _odd_mixer(xs, l, i, norm_g, mod_c, mod_l, od_w_in, od_w_out, na_q_norm, na_k_norm, na_rpb, ml_gate_b, ml_norm,
               *, batch, nt):
    w = od_w_in[i]
    w = jnp.concatenate([w, jnp.zeros((w.shape[0], OD_PAD_COLS - w.shape[1]), w.dtype)], axis=1).astype(BF16)
    u = _modmm(xs, norm_g, mod_c, mod_l, w, nt=nt, sh_idx=0, sc_idx=1, tn=896)
    bias = _na_bias_table(na_rpb[i], (nt - CTX_LEN) // GRID_W)
    o_na = _na(u, na_q_norm[i].reshape(1, -1), na_k_norm[i].reshape(1, -1), bias, batch=batch, nt=nt)
    hf, hb = _mlstm(u, _pad_lane(ml_gate_b[i]), batch=batch, nt=nt)
    o_ml = _mlstm_out(hf, hb, u, ml_norm[i].reshape(1, -1))
    return _outproj(o_na, o_ml, od_w_out[i].astype(BF16), xs, mod_c, mod_l, nt=nt, g_idx=2)


def kernel(x, c, ctx, c_ctx, mod_w, mod_b, norm_mix, norm_ffn, ev_w_in, ev_w_out, mla_ln_q, mla_ln_kv, mla_w_uq,
           mla_w_ukv, mla_q_norm, mla_k_norm, ssm_conv_w, ssm_conv_b, ssm_a_log, ssm_dt_bias, ssm_d, ssm_norm,
           od_w_in, od_w_out, na_q_norm, na_k_norm, na_rpb, ml_gate_b, ml_norm, router_w, router_b, moe_w1, moe_b1,
           moe_w2, moe_b2):
    batch, seq, d = x.shape
    n_ctx = ctx.shape[1]
    assert n_ctx == CTX_LEN and d == D_MODEL and seq % GRID_W == 0
    nt = n_ctx + seq
    xs = jnp.concatenate([ctx, x], axis=1).reshape(batch * nt, d)
    cond = jnp.concatenate([c, c_ctx[None, :], jnp.zeros((8 - batch - 1, d), F32)], axis=0)
    mods = _mods(cond, mod_w, mod_b).reshape(DEPTH, 8, 6, d)
    cos, sin = _rope_tables(seq)
    perm = _swiglu_perm()
    for l in range(DEPTH):
        i = l // 2
        mod_l = mods[l, :batch]
        mod_c = mods[l, batch]
        if l % 2 == 0:
            xs = _even_mixer(xs, l, i, norm_mix[l], mod_c, mod_l, ev_w_in, ev_w_out, mla_ln_q, mla_ln_kv, mla_w_uq,
                             mla_w_ukv, mla_q_norm, mla_k_norm, ssm_conv_w, ssm_conv_b, ssm_a_log, ssm_dt_bias,
                             ssm_d, ssm_norm, cos, sin, batch=batch, nt=nt)
        else:
            xs = _odd_mixer(xs, l, i, norm_mix[l], mod_c, mod_l, od_w_in, od_w_out, na_q_norm, na_k_norm, na_rpb,
                            ml_gate_b, ml_norm, batch=batch, nt=nt)
        skip = 1 if l == DEPTH - 1 else 0
        xs = _moe_layer(xs, l, norm_ffn[l], mod_c, mod_l, router_w, router_b, moe_w1, moe_b1, moe_w2, moe_b2, perm,
                        batch=batch, nt=nt, skip_tiles=skip)
    return xs.reshape(batch, seq, d)
```
